```python
import jax, jax.numpy as jnp
from jax import lax
import numpy as np

D_MODEL = 2048
BATCH = 2
SEQ = 4096
DEPTH = 4

GRID_W = 64
CTX_LEN = 256

D_MIX = D_MODEL
MLA_HEADS = 8
QK_NOPE = 128
QK_ROPE = 64
QK_HEAD = QK_NOPE + QK_ROPE
V_HEAD = 128
Q_RANK = 512
KV_RANK = 256
MLA_WIDTH = MLA_HEADS * V_HEAD
LRU_WIDTH = D_MIX - MLA_WIDTH
LRU_HEADS = 8
LRU_HEAD_DIM = LRU_WIDTH // LRU_HEADS
CONV_W = 4
LRU_C = 8.0
IN_COLS = Q_RANK + KV_RANK + QK_ROPE + 2 * LRU_WIDTH
IN_SPLITS = (Q_RANK, Q_RANK + KV_RANK, Q_RANK + KV_RANK + QK_ROPE, Q_RANK + KV_RANK + QK_ROPE + LRU_WIDTH)
N_EXPERTS = 32
N_GROUPS = 4
EXPERTS_PER_GROUP = N_EXPERTS // N_GROUPS
GROUP_SCORE_TOPK = 2
TOP_K = 2
D_FF_EXPERT = 768
MOE_BLOCK = 128
Q_BLOCK = 128
ROPE_BASE = 10000.0
EPS = 1e-6

kernel_name = 'hybrid_mla_rglru_grouped_moe_dit'


def rms_norm(x, gain=None):
    xf = x.astype(jnp.float32)
    y = (xf * lax.rsqrt(jnp.mean(xf * xf, axis=-1, keepdims=True) + EPS)).astype(x.dtype)
    return y if gain is None else y * gain


def modulate(x, shift, scale):
    return rms_norm(x) * (1 + scale) + shift


def axial_rope_angles(n):
    rows = n // GRID_W
    row = jnp.repeat(jnp.arange(rows), GRID_W).astype(jnp.float32)
    col = jnp.tile(jnp.arange(GRID_W), rows).astype(jnp.float32)
    n_freq = QK_ROPE // 4
    inv_freq = ROPE_BASE ** (-jnp.arange(n_freq, dtype=jnp.float32) / n_freq)
    return row[:, None] * inv_freq[None, :], col[:, None] * inv_freq[None, :]


def rotate(x, ang):
    cos = jnp.cos(ang)[None, :, None, :].astype(x.dtype)
    sin = jnp.sin(ang)[None, :, None, :].astype(x.dtype)
    x1, x2 = jnp.split(x, 2, axis=-1)
    return jnp.concatenate([x1 * cos - x2 * sin, x1 * sin + x2 * cos], axis=-1)


def apply_axial_rope(x, ang_row, ang_col):
    x_row, x_col = jnp.split(x, 2, axis=-1)
    return jnp.concatenate([rotate(x_row, ang_row), rotate(x_col, ang_col)], axis=-1)


def mla_queries(cq, q_a_norm, w_uq, q_norm, rope):
    B, n, _ = cq.shape
    q = (rms_norm(cq, q_a_norm) @ w_uq).reshape(B, n, MLA_HEADS, QK_HEAD)
    q_nope = rms_norm(q[..., :QK_NOPE], q_norm[:QK_NOPE])
    q_rope = rms_norm(q[..., QK_NOPE:], q_norm[QK_NOPE:])
    if rope is not None:
        q_rope = apply_axial_rope(q_rope, *rope)
    return jnp.concatenate([q_nope, q_rope], axis=-1)


def mla_keys_values(ckv, kr, kv_a_norm, w_ukv, k_norm, rope):
    B, n, _ = ckv.shape
    kv = (rms_norm(ckv, kv_a_norm) @ w_ukv).reshape(B, n, MLA_HEADS, QK_NOPE + V_HEAD)
    k_nope = rms_norm(kv[..., :QK_NOPE], k_norm[:QK_NOPE])
    k_rope = rms_norm(kr[:, :, None, :], k_norm[QK_NOPE:])
    if rope is not None:
        k_rope = apply_axial_rope(k_rope, *rope)
    k = jnp.concatenate([k_nope, jnp.broadcast_to(k_rope, (B, n, MLA_HEADS, QK_ROPE))], axis=-1)
    return k, kv[..., QK_NOPE:]


def attend(q, k, v):
    s = jnp.einsum('bqhd,bkhd->bhqk', q, k).astype(jnp.float32) * (QK_HEAD ** -0.5)
    p = jax.nn.softmax(s, axis=-1).astype(v.dtype)
    return jnp.einsum('bhqk,bkhd->bqhd', p, v)


def blocked_attend(q, k, v):
    B, n, H, Dk = q.shape
    nb = n // Q_BLOCK
    qb = q.reshape(B, nb, Q_BLOCK, H, Dk).transpose(1, 0, 2, 3, 4)
    ob = lax.map(lambda qi: attend(qi, k, v), qb)
    return ob.transpose(1, 0, 2, 3, 4).reshape(B, n, H, v.shape[-1])


def centred_dwconv(x, w, b):
    n = x.shape[1]
    left = CONV_W // 2
    xp = jnp.pad(x, ((0, 0), (left, CONV_W - 1 - left), (0, 0)))
    return b + sum(xp[:, k:k + n] * w[k] for k in range(CONV_W))


def rglru_coeffs(u, wa, ba, wx, bx, lam):
    B, n, _ = u.shape
    f32 = jnp.float32
    uh = u.reshape(B, n, LRU_HEADS, LRU_HEAD_DIM)
    r = jax.nn.sigmoid(jnp.einsum('bnhi,hij->bnhj', uh, wa.astype(f32)).reshape(B, n, LRU_WIDTH) + ba.astype(f32))
    i = jax.nn.sigmoid(jnp.einsum('bnhi,hij->bnhj', uh, wx.astype(f32)).reshape(B, n, LRU_WIDTH) + bx.astype(f32))
    log_a = -LRU_C * r * jax.nn.softplus(-lam.astype(f32))
    return jnp.exp(log_a), jnp.sqrt(-jnp.expm1(2.0 * log_a)) * (i * u)


def linear_scan(a, b, h0, reverse):
    if h0 is not None:
        edge = -1 if reverse else 0
        b = b.at[:, edge].add(a[:, edge] * h0)

    def combine(left, right):
        a_l, b_l = left
        a_r, b_r = right
        return a_l * a_r, a_r * b_l + b_r

    return lax.associative_scan(combine, (a, b), reverse=reverse, axis=1)[1]


def rglru_bidirectional(xr_c, xr_l, conv_w, conv_b, lru_wa, lru_ba, lru_wx, lru_bx, lru_lambda):
    uc = centred_dwconv(xr_c, conv_w, conv_b).astype(jnp.float32)
    ul = centred_dwconv(xr_l, conv_w, conv_b).astype(jnp.float32)
    h_lat, h_ctx = [], []
    for d, reverse in enumerate((False, True)):
        ac, bc = rglru_coeffs(uc, lru_wa[d], lru_ba[d], lru_wx[d], lru_bx[d], lru_lambda[d])
        hc = linear_scan(ac, bc, None, reverse)
        h_end = hc[:, 0] if reverse else hc[:, -1]
        al, bl = rglru_coeffs(ul, lru_wa[d], lru_ba[d], lru_wx[d], lru_bx[d], lru_lambda[d])
        h_lat.append(linear_scan(al, bl, h_end, reverse))
        h_ctx.append(hc)
    return (h_lat[0] + h_lat[1]).astype(xr_l.dtype), h_ctx


def merge_groups(attn, rec, out_norm, w_out):
    y = jnp.concatenate([rms_norm(attn, out_norm[:MLA_WIDTH]), rms_norm(rec, out_norm[MLA_WIDTH:])], axis=-1)
    return y @ w_out


def token_mixer(hc, hl, rope, w_in, q_a_norm, w_uq, kv_a_norm, w_ukv, q_norm, k_norm, conv_w, conv_b,
                lru_wa, lru_ba, lru_wx, lru_bx, lru_lambda, out_norm, w_out, ctx_out):
    B, n, _ = hl.shape
    cq_c, ckv_c, kr_c, xr_c, xg_c = jnp.split(hc @ w_in, IN_SPLITS, axis=-1)
    cq_l, ckv_l, kr_l, xr_l, xg_l = jnp.split(hl @ w_in, IN_SPLITS, axis=-1)
    kc, vc = mla_keys_values(ckv_c, kr_c, kv_a_norm, w_ukv, k_norm, None)
    kl, vl = mla_keys_values(ckv_l, kr_l, kv_a_norm, w_ukv, k_norm, rope)
    ql = mla_queries(cq_l, q_a_norm, w_uq, q_norm, rope)
    attn_l = blocked_attend(ql, jnp.concatenate([kc, kl], axis=1),
                            jnp.concatenate([vc, vl], axis=1)).reshape(B, n, MLA_WIDTH)
    rec_l, h_ctx = rglru_bidirectional(xr_c, xr_l, conv_w, conv_b, lru_wa, lru_ba, lru_wx, lru_bx, lru_lambda)
    yl = merge_groups(attn_l, jax.nn.gelu(xg_l) * rec_l, out_norm, w_out)
    if not ctx_out:
        return yl, None
    qc = mla_queries(cq_c, q_a_norm, w_uq, q_norm, None)
    attn_c = attend(qc, kc, vc).reshape(hc.shape[0], hc.shape[1], MLA_WIDTH)
    rec_c = (h_ctx[0] + h_ctx[1]).astype(hc.dtype)
    yc = merge_groups(attn_c, jax.nn.gelu(xg_c) * rec_c, out_norm, w_out)
    return yl, yc


def moe_ffn(h, router_w, router_b, w_gate, w_up, w_down):
    T, D = h.shape
    scores = jax.nn.sigmoid((h @ router_w).astype(jnp.float32))
    biased = (scores + router_b.astype(jnp.float32)).reshape(T, N_GROUPS, EXPERTS_PER_GROUP)
    group_score = lax.top_k(biased, GROUP_SCORE_TOPK)[0].sum(-1)
    g_sel = jnp.argmax(group_score, axis=-1)
    in_group = jnp.take_along_axis(biased, g_sel[:, None, None], axis=1)[:, 0]
    _, local = lax.top_k(in_group, TOP_K)
    expert = g_sel[:, None] * EXPERTS_PER_GROUP + local
    gate = jnp.take_along_axis(scores, expert, axis=1)
    gate = gate / gate.sum(-1, keepdims=True)
    n_pairs = T * TOP_K
    flat_e = expert.reshape(-1)
    flat_tok = jnp.repeat(jnp.arange(T), TOP_K)
    order = jnp.argsort(flat_e, stable=True)
    se, stok = flat_e[order], flat_tok[order]
    sgate = gate.reshape(-1)[order]
    counts = jnp.zeros((N_EXPERTS,), jnp.int32).at[flat_e].add(1)
    start = jnp.cumsum(counts) - counts
    padded = (counts + MOE_BLOCK - 1) // MOE_BLOCK * MOE_BLOCK
    pend = jnp.cumsum(padded)
    slot = (pend - padded)[se] + jnp.arange(n_pairs) - start[se]
    n_blocks = -(-n_pairs // MOE_BLOCK) + N_EXPERTS
    buf = jnp.zeros((n_blocks * MOE_BLOCK, D), h.dtype).at[slot].set(h[stok])
    block_expert = jnp.minimum(jnp.searchsorted(pend, jnp.arange(n_blocks) * MOE_BLOCK, side='right'),
                               N_EXPERTS - 1)

    def expert_block(args):
        xb, e = args
        return (jax.nn.silu(xb @ w_gate[e]) * (xb @ w_up[e])) @ w_down[e]

    out_buf = lax.map(expert_block, (buf.reshape(n_blocks, MOE_BLOCK, D), block_expert))
    y = out_buf.reshape(-1, D)[slot] * sgate[:, None].astype(h.dtype)
    return jax.ops.segment_sum(y, stok, num_segments=T)


def setup_inputs(seed: int = 0) -> dict:
    key = jax.random.key(seed)
    ks = jax.random.split(key, 27)

    def normal(k, shape, scale):
        return scale * jax.random.normal(k, shape, jnp.float32)

    a0 = jax.random.uniform(ks[19], (DEPTH, 2, LRU_WIDTH), jnp.float32, 0.9, 0.999)
    p0 = a0 ** (1.0 / LRU_C)
    return {
        'x': normal(ks[0], (BATCH, SEQ, D_MODEL), 1.0),
        'c': normal(ks[1], (BATCH, D_MODEL), 1.0),
        'ctx': normal(ks[2], (BATCH, CTX_LEN, D_MODEL), 1.0),
        'c_ctx': normal(ks[3], (D_MODEL,), 1.0),
        'ada_w': normal(ks[4], (DEPTH, D_MODEL, 6 * D_MODEL), 0.5 * D_MODEL ** -0.5),
        'ada_b': normal(ks[5], (DEPTH, 6 * D_MODEL), 0.01),
        'w_in': normal(ks[6], (DEPTH, D_MODEL, IN_COLS), D_MODEL ** -0.5),
        'q_a_norm': 1.0 + normal(ks[7], (DEPTH, Q_RANK), 0.02),
        'w_uq': normal(ks[8], (DEPTH, Q_RANK, MLA_HEADS * QK_HEAD), Q_RANK ** -0.5),
        'kv_a_norm': 1.0 + normal(ks[9], (DEPTH, KV_RANK), 0.02),
        'w_ukv': normal(ks[10], (DEPTH, KV_RANK, MLA_HEADS * (QK_NOPE + V_HEAD)), KV_RANK ** -0.5),
        'q_norm': 1.0 + normal(ks[11], (DEPTH, QK_HEAD), 0.02),
        'k_norm': 1.0 + normal(ks[12], (DEPTH, QK_HEAD), 0.02),
        'conv_w': normal(ks[13], (DEPTH, CONV_W, LRU_WIDTH), CONV_W ** -0.5),
        'conv_b': normal(ks[14], (DEPTH, LRU_WIDTH), 0.01),
        'lru_wa': normal(ks[15], (DEPTH, 2, LRU_HEADS, LRU_HEAD_DIM, LRU_HEAD_DIM), LRU_HEAD_DIM ** -0.5),
        'lru_ba': normal(ks[16], (DEPTH, 2, LRU_WIDTH), 0.01),
        'lru_wx': normal(ks[17], (DEPTH, 2, LRU_HEADS, LRU_HEAD_DIM, LRU_HEAD_DIM), LRU_HEAD_DIM ** -0.5),
        'lru_bx': normal(ks[18], (DEPTH, 2, LRU_WIDTH), 0.01),
        'lru_lambda': jnp.log(p0) - jnp.log1p(-p0),
        'out_norm': 1.0 + normal(ks[20], (DEPTH, D_MIX), 0.02),
        'w_out': normal(ks[21], (DEPTH, D_MIX, D_MODEL), D_MIX ** -0.5),
        'router_w': normal(ks[22], (D_MODEL, N_EXPERTS), D_MODEL ** -0.5),
        'router_b': normal(ks[23], (N_EXPERTS,), 0.01),
        'moe_w_gate': normal(ks[24], (DEPTH, N_EXPERTS, D_MODEL, D_FF_EXPERT), D_MODEL ** -0.5),
        'moe_w_up': normal(ks[25], (DEPTH, N_EXPERTS, D_MODEL, D_FF_EXPERT), D_MODEL ** -0.5),
        'moe_w_down': normal(ks[26], (DEPTH, N_EXPERTS, D_FF_EXPERT, D_MODEL), D_FF_EXPERT ** -0.5),
    }


def reference(x, c, ctx, c_ctx, ada_w, ada_b, w_in, q_a_norm, w_uq, kv_a_norm, w_ukv, q_norm, k_norm,
              conv_w, conv_b, lru_wa, lru_ba, lru_wx, lru_bx, lru_lambda, out_norm, w_out,
              router_w, router_b, moe_w_gate, moe_w_up, moe_w_down):
    B, n, D = x.shape
    rope = axial_rope_angles(n)
    silu_c = jax.nn.silu(c)
    silu_cc = jax.nn.silu(c_ctx)[None]
    xl, xc = x, ctx
    for l in range(DEPTH):
        last = l == DEPTH - 1
        mod_l = jnp.split((silu_c @ ada_w[l] + ada_b[l])[:, None, :], 6, axis=-1)
        mod_c = jnp.split((silu_cc @ ada_w[l] + ada_b[l])[:, None, :], 6, axis=-1)
        hl = modulate(xl, mod_l[0], mod_l[1])
        hc = modulate(xc, mod_c[0], mod_c[1])
        yl, yc = token_mixer(hc, hl, rope, w_in[l], q_a_norm[l], w_uq[l], kv_a_norm[l], w_ukv[l],
                             q_norm[l], k_norm[l], conv_w[l], conv_b[l], lru_wa[l], lru_ba[l],
                             lru_wx[l], lru_bx[l], lru_lambda[l], out_norm[l], w_out[l], not last)
        xl = xl + mod_l[2] * yl
        hl = modulate(xl, mod_l[3], mod_l[4])
        if last:
            y = moe_ffn(hl.reshape(-1, D), router_w, router_b, moe_w_gate[l], moe_w_up[l], moe_w_down[l])
            xl = xl + mod_l[5] * y.reshape(B, n, D)
        else:
            xc = xc + mod_c[2] * yc
            hc = modulate(xc, mod_c[3], mod_c[4])
            n_ctx = hc.shape[0] * hc.shape[1]
            y = moe_ffn(jnp.concatenate([hc.reshape(-1, D), hl.reshape(-1, D)], axis=0),
                        router_w, router_b, moe_w_gate[l], moe_w_up[l], moe_w_down[l])
            xc = xc + mod_c[5] * y[:n_ctx].reshape(hc.shape)
            xl = xl + mod_l[5] * y[n_ctx:].reshape(B, n, D)
    return xl
```

```python
import functools

import jax
import jax.numpy as jnp
from jax import lax
from jax.experimental import pallas as pl
from jax.experimental.pallas import tpu as pltpu

F32 = jnp.float32
BF16 = jnp.bfloat16

D_MODEL = 2048
MLA_HEADS = 8
QK_NOPE = 128
QK_ROPE = 64
QK_HEAD = QK_NOPE + QK_ROPE
V_HEAD = 128
Q_RANK = 512
KV_RANK = 256
MLA_WIDTH = MLA_HEADS * V_HEAD
LRU_WIDTH = D_MODEL - MLA_WIDTH
LRU_HEADS = 8
LRU_HEAD_DIM = LRU_WIDTH // LRU_HEADS
CONV_W = 4
LRU_C = 8.0
N_EXPERTS = 32
N_GROUPS = 4
EXPERTS_PER_GROUP = N_EXPERTS // N_GROUPS
D_FF_EXPERT = 768
GRID_W = 64
ROPE_BASE = 10000.0
EPS = 1e-6

LANES = 128
SUBLANES = 8
VMEM_LIMIT = 56 * 1024 * 1024

TM = 256
QK_PAD = 2 * LANES
MLA_COLS = Q_RANK + KV_RANK + LANES
ATTN_HEADS_PER_STEP = 4
ATTN_KV_CHUNK = 512
MOE_TILE = 256
ROUTE_LANES = LANES

_NT = (((1,), (1,)), ((), ()))


def _dot(a, b):
    return jnp.dot(a, b, preferred_element_type=F32)


def _rms(x, denom=None):
    d = x.shape[-1] if denom is None else denom
    ms = jnp.sum(x * x, axis=-1, keepdims=True) * (1.0 / d)
    return x * lax.rsqrt(ms + EPS)


def _params(*sem):
    return pltpu.CompilerParams(dimension_semantics=sem, vmem_limit_bytes=VMEM_LIMIT)


def _resident(shape):
    nd = len(shape)
    return pl.BlockSpec(shape, lambda *_: (0,) * nd, pipeline_mode=pl.Buffered(1))


def _adaln_kernel(c_ref, w_ref, b_ref, o_ref):
    s = jax.nn.silu(c_ref[...]).astype(BF16)
    o_ref[0] = _dot(s, w_ref[0].astype(BF16)) + b_ref[0]


def _adaln(cvec, ada_w, ada_b):
    depth, d, n6 = ada_w.shape
    rows = cvec.shape[0]
    tn = 1024
    return pl.pallas_call(
        _adaln_kernel,
        grid=(depth, n6 // tn),
        in_specs=[
            pl.BlockSpec((rows, d), lambda l, j: (0, 0)),
            pl.BlockSpec((1, d, tn), lambda l, j: (l, 0, j)),
            pl.BlockSpec((1, 1, tn), lambda l, j: (l, 0, j)),
        ],
        out_specs=pl.BlockSpec((1, rows, tn), lambda l, j: (l, 0, j)),
        out_shape=jax.ShapeDtypeStruct((depth, rows, n6), F32),
        compiler_params=_params("arbitrary", "arbitrary"),
        name="adaln",
    )(cvec, ada_w, ada_b.reshape(depth, 1, n6))


def _seg(i, tiles_per_batch):
    return jnp.where(i % tiles_per_batch == 0, 0, 1 + i // tiles_per_batch)


def _mod_spec(chunk, tiles_per_batch):
    return pl.BlockSpec((1, 1, D_MODEL), lambda i: (_seg(i, tiles_per_batch) * 6 + chunk, 0, 0))


def _inproj_kernel(x_ref, shift_ref, scale_ref, w_ref, lru_ref, mla_ref):
    h = _rms(x_ref[...]) * (1.0 + scale_ref[0]) + shift_ref[0]
    y = _dot(h.astype(BF16), w_ref[...])
    lru_ref[...] = y[:, :2 * LRU_WIDTH]
    mla_ref[...] = y[:, 2 * LRU_WIDTH:]


def _inproj(x, mod, w_in_p, tpb):
    t = x.shape[0]
    n_out = w_in_p.shape[1]
    return pl.pallas_call(
        _inproj_kernel,
        grid=(t // TM,),
        in_specs=[
            pl.BlockSpec((TM, D_MODEL), lambda i: (i, 0)),
            _mod_spec(0, tpb),
            _mod_spec(1, tpb),
            _resident((D_MODEL, n_out)),
        ],
        out_specs=[
            pl.BlockSpec((TM, 2 * LRU_WIDTH), lambda i: (i, 0)),
            pl.BlockSpec((TM, MLA_COLS), lambda i: (i, 0)),
        ],
        out_shape=[
            jax.ShapeDtypeStruct((t, 2 * LRU_WIDTH), F32),
            jax.ShapeDtypeStruct((t, MLA_COLS), F32),
        ],
        compiler_params=_params("arbitrary"),
        name="inproj",
    )(x, mod, mod, w_in_p)


def _swap_halves(x):
    lane = lax.broadcasted_iota(jnp.int32, x.shape, 1)
    up = pltpu.roll(x, LANES - 16, 1)
    down = pltpu.roll(x, 16, 1)
    return jnp.where((lane & 31) < 16, up, down)


def _mla_prep_kernel(p_ref, qa_ref, kva_ref, qn_ref, kn_ref, cos_ref, sin_ref, wuq_ref, wukv_ref,
                     q_ref, k_ref, v_ref):
    cos = cos_ref[...]
    sin = sin_ref[...]
    scale = QK_HEAD ** -0.5

    def rope(x):
        return x * cos + _swap_halves(x) * sin

    cq = p_ref[:, :Q_RANK]
    qf = _dot((_rms(cq) * qa_ref[...]).astype(BF16), wuq_ref[...])
    ckv = p_ref[:, Q_RANK:Q_RANK + KV_RANK]
    kvf = _dot((_rms(ckv) * kva_ref[...]).astype(BF16), wukv_ref[...])
    kr = p_ref[:, Q_RANK + KV_RANK:]
    k_rope = rope(_rms(kr, QK_ROPE) * kn_ref[:, QK_NOPE:]).astype(BF16)
    for h in range(MLA_HEADS):
        c0 = h * QK_PAD
        q_nope = _rms(qf[:, c0:c0 + QK_NOPE]) * qn_ref[:, :QK_NOPE]
        q_rope = rope(_rms(qf[:, c0 + QK_NOPE:c0 + QK_PAD], QK_ROPE) * qn_ref[:, QK_NOPE:])
        q_ref[:, c0:c0 + QK_NOPE] = (q_nope * scale).astype(BF16)
        q_ref[:, c0 + QK_NOPE:c0 + QK_PAD] = (q_rope * scale).astype(BF16)
        k_nope = _rms(kvf[:, h * QK_NOPE:(h + 1) * QK_NOPE]) * kn_ref[:, :QK_NOPE]
        k_ref[:, c0:c0 + QK_NOPE] = k_nope.astype(BF16)
        k_ref[:, c0 + QK_NOPE:c0 + QK_PAD] = k_rope
        v0 = MLA_HEADS * QK_NOPE + h * V_HEAD
        v_ref[:, h * V_HEAD:(h + 1) * V_HEAD] = kvf[:, v0:v0 + V_HEAD].astype(BF16)


def _mla_prep(mla_in, qa, kva, qn, kn, cos_t, sin_t, wuq_p, wukv_p, tpb):
    t = mla_in.shape[0]
    row = lambda n: pl.BlockSpec((1, n), lambda i: (0, 0))
    return pl.pallas_call(
        _mla_prep_kernel,
        grid=(t // TM,),
        in_specs=[
            pl.BlockSpec((TM, MLA_COLS), lambda i: (i, 0)),
            row(Q_RANK), row(KV_RANK), row(QK_PAD), row(QK_PAD),
            pl.BlockSpec((TM, LANES), lambda i: (i % tpb, 0)),
            pl.BlockSpec((TM, LANES), lambda i: (i % tpb, 0)),
            _resident(wuq_p.shape),
            _resident(wukv_p.shape),
        ],
        out_specs=[
            pl.BlockSpec((TM, MLA_HEADS * QK_PAD), lambda i: (i, 0)),
            pl.BlockSpec((TM, MLA_HEADS * QK_PAD), lambda i: (i, 0)),
            pl.BlockSpec((TM, MLA_WIDTH), lambda i: (i, 0)),
        ],
        out_shape=[
            jax.ShapeDtypeStruct((t, MLA_HEADS * QK_PAD), BF16),
            jax.ShapeDtypeStruct((t, MLA_HEADS * QK_PAD), BF16),
            jax.ShapeDtypeStruct((t, MLA_WIDTH), BF16),
        ],
        compiler_params=_params("arbitrary"),
        name="mla_prep",
    )(mla_in, qa, kva, qn, kn, cos_t, sin_t, wuq_p, wukv_p)


def _attn_kernel(q_ref, k_ref, v_ref, o_ref, *, heads, n_chunks):
    trips = jnp.where(pl.program_id(2) == 0, 0, n_chunks)
    for j in range(heads):
        q = q_ref[:, j * QK_PAD:(j + 1) * QK_PAD]

        def scores(start, size):
            k = k_ref[pl.ds(start, size), j * QK_PAD:(j + 1) * QK_PAD]
            return lax.dot_general(q, k, _NT, preferred_element_type=F32)

        def values(start, size):
            return v_ref[pl.ds(start, size), j * V_HEAD:(j + 1) * V_HEAD]

        s = scores(0, TM)
        m = jnp.max(s, axis=-1, keepdims=True)
        e = jnp.exp(s - m)
        l = jnp.sum(e, axis=-1, keepdims=True)
        acc = _dot(e.astype(BF16), values(0, TM))

        def body(c, carry):
            m, l, acc = carry
            start = pl.multiple_of(TM + c * ATTN_KV_CHUNK, ATTN_KV_CHUNK // 2)
            s = scores(start, ATTN_KV_CHUNK)
            m_new = jnp.maximum(m, jnp.max(s, axis=-1, keepdims=True))
            alpha = jnp.exp(m - m_new)
            e = jnp.exp(s - m_new)
            l = alpha * l + jnp.sum(e, axis=-1, keepdims=True)
            acc = alpha * acc + _dot(e.astype(BF16), values(start, ATTN_KV_CHUNK))
            return m_new, l, acc

        m, l, acc = lax.fori_loop(0, trips, body, (m, l, acc))
        o_ref[:, j * V_HEAD:(j + 1) * V_HEAD] = acc / l


def _attention(q, k, v, batch, tpb):
    t = q.shape[0]
    rows = tpb * TM
    hb = ATTN_HEADS_PER_STEP
    n_chunks = (rows - TM) // ATTN_KV_CHUNK
    return pl.pallas_call(
        functools.partial(_attn_kernel, heads=hb, n_chunks=n_chunks),
        grid=(batch, MLA_HEADS // hb, tpb),
        in_specs=[
            pl.BlockSpec((TM, hb * QK_PAD), lambda b, g, i: (b * tpb + i, g)),
            pl.BlockSpec((rows, hb * QK_PAD), lambda b, g, i: (b, g)),
            pl.BlockSpec((rows, hb * V_HEAD), lambda b, g, i: (b, g)),
        ],
        out_specs=pl.BlockSpec((TM, hb * V_HEAD), lambda b, g, i: (b * tpb + i, g)),
        out_shape=jax.ShapeDtypeStruct((t, MLA_WIDTH), F32),
        compiler_params=_params("arbitrary", "arbitrary", "arbitrary"),
        name="attention",
    )(q, k, v)


def _shift_rows(x, s, fill, reverse):
    rows = x.shape[0]
    pad = jnp.full((s, x.shape[1]), fill, x.dtype)
    if s % SUBLANES == 0:
        return jnp.concatenate([x[s:], pad], 0) if reverse else jnp.concatenate([pad, x[:rows - s]], 0)
    row = lax.broadcasted_iota(jnp.int32, x.shape, 0)
    if reverse:
        return jnp.where(row < rows - s, pltpu.roll(x, rows - s, 0), fill)
    return jnp.where(row >= s, pltpu.roll(x, s, 0), fill)


def _lru_kernel(*refs, reverse, tpb):
    if reverse:
        (x_ref, prev_ref, next_ref, cw_ref, cb_ref, w_ref, ba_ref, bx_ref, lam_ref, hf_ref, xg_ref,
         o_ref, carry_ref) = refs
    else:
        (x_ref, prev_ref, next_ref, cw_ref, cb_ref, w_ref, ba_ref, bx_ref, lam_ref,
         o_ref, carry_ref) = refs
    step = pl.program_id(1)
    tile = _lru_tile(step, tpb, reverse)

    @pl.when(step == 0)
    def _():
        carry_ref[...] = jnp.zeros_like(carry_ref)

    x = x_ref[...]
    row = lax.broadcasted_iota(jnp.int32, x.shape, 0)
    at_start = tile <= 1
    at_end = jnp.logical_or(tile == 0, tile == tpb - 1)
    before1 = jnp.where(at_start, 0.0, prev_ref[SUBLANES - 1:SUBLANES, :])
    before2 = jnp.where(at_start, 0.0, prev_ref[SUBLANES - 2:SUBLANES - 1, :])
    after1 = jnp.where(at_end, 0.0, next_ref[0:1, :])
    x_m1 = jnp.where(row == 0, before1, pltpu.roll(x, 1, 0))
    x_m2 = jnp.where(row == 0, before2, jnp.where(row == 1, before1, pltpu.roll(x, 2, 0)))
    x_p1 = jnp.where(row == TM - 1, after1, pltpu.roll(x, TM - 1, 0))
    u = cb_ref[...] + (((x_m2 * cw_ref[0:1, :] + x_m1 * cw_ref[1:2, :]) + x * cw_ref[2:3, :])
                       + x_p1 * cw_ref[3:4, :])

    decay = -LRU_C * jax.nn.softplus(-lam_ref[...])
    for h in range(LRU_HEADS):
        c0 = h * LRU_HEAD_DIM
        sl = slice(c0, c0 + LRU_HEAD_DIM)
        uh = u[:, sl]
        z = _dot(uh.astype(BF16), w_ref[h])
        r = jax.nn.sigmoid(z[:, :LRU_HEAD_DIM] + ba_ref[:, sl])
        i = jax.nn.sigmoid(z[:, LRU_HEAD_DIM:] + bx_ref[:, sl])
        log_a = r * decay[:, sl]
        a = jnp.exp(log_a)
        b = jnp.sqrt(-jnp.tanh(log_a) * (1.0 + a * a)) * (i * uh)
        s = 1
        while s < TM:
            b = a * _shift_rows(b, s, 0.0, reverse) + b
            a = a * _shift_rows(a, s, 1.0, reverse)
            s *= 2
        hcur = b + a * carry_ref[0:1, sl]
        edge = 0 if reverse else TM - 1
        carry_ref[0:1, sl] = hcur[edge:edge + 1, :]
        if reverse:
            o_ref[:, sl] = jax.nn.gelu(xg_ref[:, sl]) * (hf_ref[:, sl] + hcur)
        else:
            o_ref[:, sl] = hcur


def _lru_tile(step, tpb, reverse):
    if not reverse:
        return step
    return jnp.where(step == 0, 0, tpb - step)


def _lru(lru_in, conv_w, conv_b, w_gate, ba, bx, lam, batch, tpb, reverse, h_fwd=None):
    t = lru_in.shape[0]
    halo_blocks = t // SUBLANES
    per_tile = TM // SUBLANES

    def tile_idx(b, s):
        return b * tpb + _lru_tile(s, tpb, reverse)

    row = lambda n: pl.BlockSpec((1, n), lambda b, s: (0, 0))
    in_specs = [
        pl.BlockSpec((TM, LRU_WIDTH), lambda b, s: (tile_idx(b, s), 0)),
        pl.BlockSpec((SUBLANES, LRU_WIDTH), lambda b, s: (jnp.maximum(tile_idx(b, s) * per_tile - 1, 0), 0)),
        pl.BlockSpec((SUBLANES, LRU_WIDTH),
                     lambda b, s: (jnp.minimum((tile_idx(b, s) + 1) * per_tile, halo_blocks - 1), 0)),
        pl.BlockSpec((CONV_W, LRU_WIDTH), lambda b, s: (0, 0)),
        row(LRU_WIDTH),
        pl.BlockSpec((LRU_HEADS, LRU_HEAD_DIM, 2 * LRU_HEAD_DIM), lambda b, s: (0, 0, 0)),
        row(LRU_WIDTH), row(LRU_WIDTH), row(LRU_WIDTH),
    ]
    args = [lru_in, lru_in, lru_in, conv_w, conv_b, w_gate, ba, bx, lam]
    if reverse:
        in_specs += [
            pl.BlockSpec((TM, LRU_WIDTH), lambda b, s: (tile_idx(b, s), 0)),
            pl.BlockSpec((TM, LRU_WIDTH), lambda b, s: (tile_idx(b, s), 1)),
        ]
        args += [h_fwd, lru_in]
    return pl.pallas_call(
        functools.partial(_lru_kernel, reverse=reverse, tpb=tpb),
        grid=(batch, tpb),
        in_specs=in_specs,
        out_specs=pl.BlockSpec((TM, LRU_WIDTH), lambda b, s: (tile_idx(b, s), 0)),
        out_shape=jax.ShapeDtypeStruct((t, LRU_WIDTH), F32),
        scratch_shapes=[pltpu.VMEM((SUBLANES, LRU_WIDTH), F32)],
        compiler_params=_params("arbitrary", "arbitrary"),
        name="lru_rev" if reverse else "lru_fwd",
    )(*args)


def _first_index(mask, lane_f):
    return jnp.min(jnp.where(mask, lane_f, float(ROUTE_LANES)), axis=-1, keepdims=True)


def _route(logits, rb):
    lane = lax.broadcasted_iota(jnp.int32, logits.shape, 1)
    lane_f = lane.astype(F32)
    group = lane >> 3
    scores = jax.nn.sigmoid(logits)
    biased = scores + rb
    neg = -jnp.inf
    best = e1 = e2 = None
    for g in range(N_GROUPS):
        vg = jnp.where(group == g, biased, neg)
        m1 = jnp.max(vg, axis=-1, keepdims=True)
        i1 = _first_index(vg == m1, lane_f)
        vg2 = jnp.where(lane_f == i1, neg, vg)
        m2 = jnp.max(vg2, axis=-1, keepdims=True)
        i2 = _first_index(vg2 == m2, lane_f)
        gs = m1 + m2
        if g == 0:
            best, e1, e2 = gs, i1, i2
        else:
            better = gs > best
            best = jnp.where(better, gs, best)
            e1 = jnp.where(better, i1, e1)
            e2 = jnp.where(better, i2, e2)
    s1 = jnp.sum(jnp.where(lane_f == e1, scores, 0.0), axis=-1, keepdims=True)
    s2 = jnp.sum(jnp.where(lane_f == e2, scores, 0.0), axis=-1, keepdims=True)
    tot = s1 + s2
    out = jnp.where(lane == 0, e1, jnp.where(lane == 1, e2, jnp.where(lane == 2, s1 / tot, s2 / tot)))
    return jnp.where(lane < 4, out, 0.0)


def _merge_kernel(attn_ref, rec_ref, x_ref, on_ref, gate_ref, shift_ref, scale_ref, wout_ref,
                  rwh_ref, rwl_ref, rb_ref, xn_ref, h_ref, route_ref):
    ya = (_rms(attn_ref[...]) * on_ref[:, :MLA_WIDTH]).astype(BF16)
    yr = (_rms(rec_ref[...]) * on_ref[:, MLA_WIDTH:]).astype(BF16)
    y = _dot(ya, wout_ref[:MLA_WIDTH, :]) + _dot(yr, wout_ref[MLA_WIDTH:, :])
    xn = x_ref[...] + gate_ref[0] * y
    xn_ref[...] = xn
    h = _rms(xn) * (1.0 + scale_ref[0]) + shift_ref[0]
    h_hi = h.astype(BF16)
    h_ref[...] = h_hi
    h_lo = (h - h_hi.astype(F32)).astype(BF16)
    logits = (_dot(h_hi, rwh_ref[...]) + _dot(h_lo, rwh_ref[...])) + _dot(h_hi, rwl_ref[...])
    route_ref[...] = _route(logits, rb_ref[...])


def _merge(attn, rec, x, mod, out_norm, w_out, rw_hi, rw_lo, rb, tpb):
    t = x.shape[0]
    tile = lambda n: pl.BlockSpec((TM, n), lambda i: (i, 0))
    return pl.pallas_call(
        _merge_kernel,
        grid=(t // TM,),
        in_specs=[
            tile(MLA_WIDTH), tile(LRU_WIDTH), tile(D_MODEL),
            pl.BlockSpec((1, D_MODEL), lambda i: (0, 0)),
            _mod_spec(2, tpb), _mod_spec(3, tpb), _mod_spec(4, tpb),
            _resident(w_out.shape), _resident(rw_hi.shape), _resident(rw_lo.shape),
            pl.BlockSpec((1, ROUTE_LANES), lambda i: (0, 0)),
        ],
        out_specs=[tile(D_MODEL), tile(D_MODEL), tile(ROUTE_LANES)],
        out_shape=[
            jax.ShapeDtypeStruct((t, D_MODEL), F32),
            jax.ShapeDtypeStruct((t, D_MODEL), BF16),
            jax.ShapeDtypeStruct((t, ROUTE_LANES), F32),
        ],
        compiler_params=_params("arbitrary"),
        name="merge",
    )(attn, rec, x, out_norm, mod, mod, mod, w_out, rw_hi, rw_lo, rb)


def _expert_kernel(te_ref, used_ref, x_ref, wg_ref, wu_ref, wd_ref, o_ref):
    j = pl.program_id(0)

    @pl.when(j < used_ref[0])
    def _():
        x = x_ref[...]
        act = (jax.nn.silu(_dot(x, wg_ref[0])) * _dot(x, wu_ref[0])).astype(BF16)
        o_ref[...] = _dot(act, wd_ref[0])

    @pl.when(j >= used_ref[0])
    def _():
        o_ref[...] = jnp.zeros_like(o_ref)


def _experts(x_sorted, tile_expert, n_used, wg, wu, wd):
    rows = x_sorted.shape[0]
    n_tiles = rows // MOE_TILE
    grid_spec = pltpu.PrefetchScalarGridSpec(
        num_scalar_prefetch=2,
        grid=(n_tiles,),
        in_specs=[
            pl.BlockSpec((MOE_TILE, D_MODEL), lambda j, te, used: (jnp.minimum(j, used[0] - 1), 0)),
            pl.BlockSpec((1, D_MODEL, D_FF_EXPERT), lambda j, te, used: (te[j], 0, 0)),
            pl.BlockSpec((1, D_MODEL, D_FF_EXPERT), lambda j, te, used: (te[j], 0, 0)),
            pl.BlockSpec((1, D_FF_EXPERT, D_MODEL), lambda j, te, used: (te[j], 0, 0)),
        ],
        out_specs=pl.BlockSpec((MOE_TILE, D_MODEL), lambda j, te, used: (j, 0)),
    )
    return pl.pallas_call(
        _expert_kernel,
        grid_spec=grid_spec,
        out_shape=jax.ShapeDtypeStruct((rows, D_MODEL), F32),
        compiler_params=_params("arbitrary"),
        name="experts",
    )(tile_expert, n_used, x_sorted, wg, wu, wd)


def _dispatch(route):
    t = route.shape[0]
    expert = route[:, :2].astype(jnp.int32)
    gate = route[:, 2:4]
    flat_e = expert.reshape(-1)
    n_pairs = flat_e.shape[0]
    onehot = (flat_e[:, None] == jnp.arange(N_EXPERTS, dtype=jnp.int32)[None, :]).astype(jnp.int32)
    csum = jnp.cumsum(onehot, axis=0)
    rank = jnp.take_along_axis(csum, flat_e[:, None], axis=1)[:, 0] - 1
    counts = csum[-1]
    padded = (counts + MOE_TILE - 1) // MOE_TILE * MOE_TILE
    pend = jnp.cumsum(padded)
    slot = (pend - padded)[flat_e] + rank
    n_tiles = -(-n_pairs // MOE_TILE) + N_EXPERTS
    n_used = (pend[-1] // MOE_TILE).astype(jnp.int32)
    te = jnp.minimum(jnp.searchsorted(pend, jnp.arange(n_tiles, dtype=jnp.int32) * MOE_TILE, side='right'),
                     N_EXPERTS - 1).astype(jnp.int32)
    te = jnp.where(jnp.arange(n_tiles) < n_used, te, te[jnp.maximum(n_used - 1, 0)])
    tok = jnp.zeros((n_tiles * MOE_TILE,), jnp.int32).at[slot].set(jnp.arange(n_pairs, dtype=jnp.int32) // 2)
    return slot.reshape(t, 2), gate, te, n_used.reshape(1), tok


def _rope_tables(n_latent):
    rows = n_latent // GRID_W
    row = jnp.repeat(jnp.arange(rows), GRID_W).astype(F32)
    col = jnp.tile(jnp.arange(GRID_W), rows).astype(F32)
    n_freq = QK_ROPE // 4
    inv_freq = ROPE_BASE ** (-jnp.arange(n_freq, dtype=F32) / n_freq)
    ar, ac = row[:, None] * inv_freq[None, :], col[:, None] * inv_freq[None, :]
    zeros = jnp.zeros((n_latent, LANES - QK_ROPE), F32)
    cos = jnp.concatenate([jnp.cos(ar), jnp.cos(ar), jnp.cos(ac), jnp.cos(ac), zeros], axis=1)
    sin = jnp.concatenate([-jnp.sin(ar), jnp.sin(ar), -jnp.sin(ac), jnp.sin(ac), zeros], axis=1)
    ctx_cos = jnp.concatenate([jnp.ones((TM, QK_ROPE), F32), jnp.zeros((TM, LANES - QK_ROPE), F32)], axis=1)
    return (jnp.concatenate([ctx_cos, cos], axis=0),
            jnp.concatenate([jnp.zeros((TM, LANES), F32), sin], axis=0))


def _pad_cols(w, n):
    return jnp.concatenate([w, jnp.zeros(w.shape[:-1] + (n,), w.dtype)], axis=-1)


def _layer_params(l, w_in, q_a_norm, w_uq, kv_a_norm, w_ukv, q_norm, k_norm, conv_w, conv_b, lru_wa, lru_ba,
                  lru_wx, lru_bx, lru_lambda, out_norm, w_out):
    s0, s1, s2, s3 = Q_RANK, Q_RANK + KV_RANK, Q_RANK + KV_RANK + QK_ROPE, Q_RANK + KV_RANK + QK_ROPE + LRU_WIDTH
    w = w_in[l]
    w_in_p = jnp.concatenate([w[:, s2:s3], w[:, s3:], w[:, :s0], w[:, s0:s1],
                              _pad_cols(w[:, s1:s2], LANES - QK_ROPE)], axis=1).astype(BF16)
    wq = w_uq[l].reshape(Q_RANK, MLA_HEADS, QK_HEAD)
    wuq_p = _pad_cols(wq, QK_PAD - QK_HEAD).reshape(Q_RANK, MLA_HEADS * QK_PAD).astype(BF16)
    wkv = w_ukv[l].reshape(KV_RANK, MLA_HEADS, QK_NOPE + V_HEAD)
    wukv_p = jnp.concatenate([wkv[:, :, :QK_NOPE].reshape(KV_RANK, -1),
                              wkv[:, :, QK_NOPE:].reshape(KV_RANK, -1)], axis=1).astype(BF16)
    w_gate = [jnp.concatenate([lru_wa[l, d], lru_wx[l, d]], axis=-1).astype(BF16) for d in range(2)]
    return dict(
        w_in=w_in_p, wuq=wuq_p, wukv=wukv_p,
        qa=q_a_norm[l][None], kva=kv_a_norm[l][None],
        qn=_pad_cols(q_norm[l], QK_PAD - QK_HEAD)[None], kn=_pad_cols(k_norm[l], QK_PAD - QK_HEAD)[None],
        conv_w=conv_w[l], conv_b=conv_b[l][None], w_gate=w_gate,
        ba=[lru_ba[l, d][None] for d in range(2)], bx=[lru_bx[l, d][None] for d in range(2)],
        lam=[lru_lambda[l, d][None] for d in range(2)],
        out_norm=out_norm[l][None], w_out=w_out[l].astype(BF16),
    )


def kernel(x, c, ctx, c_ctx, ada_w, ada_b, w_in, q_a_norm, w_uq, kv_a_norm, w_ukv, q_norm, k_norm, conv_w, conv_b, lru_wa, lru_ba, lru_wx, lru_bx, lru_lambda, out_norm, w_out, router_w, router_b, moe_w_gate, moe_w_up, moe_w_down):
    batch, n, d = x.shape
    depth = ada_w.shape[0]
    assert d == D_MODEL and ctx.shape[1] == TM and n % ATTN_KV_CHUNK == 0 and n % GRID_W == 0
    assert 1 + batch <= SUBLANES
    tpb = (TM + n) // TM
    t = batch * tpb * TM

    cvec = jnp.concatenate([c_ctx[None], c, jnp.zeros((SUBLANES - 1 - batch, d), F32)], axis=0)
    mod_all = _adaln(cvec, ada_w, ada_b)
    cos_t, sin_t = _rope_tables(n)
    rw = _pad_cols(router_w, ROUTE_LANES - N_EXPERTS)
    rw_hi = rw.astype(BF16)
    rw_lo = (rw - rw_hi.astype(F32)).astype(BF16)
    rb = _pad_cols(router_b, ROUTE_LANES - N_EXPERTS)[None]

    xs = jnp.concatenate([ctx, x], axis=1).reshape(t, d)
    for l in range(depth):
        p = _layer_params(l, w_in, q_a_norm, w_uq, kv_a_norm, w_ukv, q_norm, k_norm, conv_w, conv_b, lru_wa,
                          lru_ba, lru_wx, lru_bx, lru_lambda, out_norm, w_out)
        mod = mod_all[l].reshape(SUBLANES * 6, 1, d)
        lru_in, mla_in = _inproj(xs, mod, p['w_in'], tpb)
        q, k, v = _mla_prep(mla_in, p['qa'], p['kva'], p['qn'], p['kn'], cos_t, sin_t, p['wuq'], p['wukv'], tpb)
        attn = _attention(q, k, v, batch, tpb)
        h_fwd = _lru(lru_in, p['conv_w'], p['conv_b'], p['w_gate'][0], p['ba'][0], p['bx'][0], p['lam'][0],
                     batch, tpb, False)
        rec = _lru(lru_in, p['conv_w'], p['conv_b'], p['w_gate'][1], p['ba'][1], p['bx'][1], p['lam'][1],
                   batch, tpb, True, h_fwd)
        xn, h2, route = _merge(attn, rec, xs, mod, p['out_norm'], p['w_out'], rw_hi, rw_lo, rb, tpb)
        slot, gate, te, n_used, tok = _dispatch(route)
        y_sorted = _experts(h2[tok], te, n_used, moe_w_gate[l].astype(BF16), moe_w_up[l].astype(BF16),
                            moe_w_down[l].astype(BF16))
        y = y_sorted[slot[:, 0]] * gate[:, 0:1] + y_sorted[slot[:, 1]] * gate[:, 1:2]
        gate2 = mod_all[l, :, 5 * d:6 * d]
        seg = _seg(jnp.arange(t // TM), tpb)
        xs = xn + jnp.repeat(gate2[seg], TM, axis=0) * y
    return xs.reshape(batch, tpb * TM, d)[:, TM:]
```

```python
import functools

import jax
import jax.numpy as jnp
from jax import lax
from jax.experimental import pallas as pl
from jax.experimental.pallas import tpu as pltpu

F32 = jnp.float32
BF16 = jnp.bfloat16
U32 = jnp.uint32

D_MODEL = 2048
MLA_HEADS = 8
QK_NOPE = 128
QK_ROPE = 64
QK_HEAD = QK_NOPE + QK_ROPE
V_HEAD = 128
Q_RANK = 512
KV_RANK = 256
MLA_WIDTH = MLA_HEADS * V_HEAD
LRU_WIDTH = D_MODEL - MLA_WIDTH
LRU_HEADS = 8
LRU_HEAD_DIM = LRU_WIDTH // LRU_HEADS
CONV_W = 4
LRU_C = 8.0
N_EXPERTS = 32
N_GROUPS = 4
EXPERTS_PER_GROUP = N_EXPERTS // N_GROUPS
D_FF_EXPERT = 768
GRID_W = 64
ROPE_BASE = 10000.0
EPS = 1e-6
LOG2_E = 1.4426950408889634

LANES = 128
SUBLANES = 8
VMEM_LIMIT = 56 * 1024 * 1024

TM = 256
QK_PAD = 2 * LANES
MLA_COLS = Q_RANK + KV_RANK + LANES
ATTN_KV_CHUNK = 512
MOE_TILE = 256
MOE_FF_CHUNK = 256
PACKED = D_MODEL // 2
ROUTE_LANES = LANES
DMA_UNROLL = 8

_NT = (((1,), (1,)), ((), ()))


def _dot(a, b):
    return jnp.dot(a, b, preferred_element_type=F32)


def _rms(x, denom=None):
    d = x.shape[-1] if denom is None else denom
    ms = jnp.sum(x * x, axis=-1, keepdims=True) * (1.0 / d)
    return x * lax.rsqrt(ms + EPS)


def _params(*sem):
    return pltpu.CompilerParams(dimension_semantics=sem, vmem_limit_bytes=VMEM_LIMIT)


def _resident(shape):
    nd = len(shape)
    return pl.BlockSpec(shape, lambda *_: (0,) * nd, pipeline_mode=pl.Buffered(1))


def _adaln_kernel(c_ref, w_ref, b_ref, o_ref):
    s = jax.nn.silu(c_ref[...]).astype(BF16)
    o_ref[0] = _dot(s, w_ref[0].astype(BF16)) + b_ref[0]


def _adaln(cvec, ada_w, ada_b):
    depth, d, n6 = ada_w.shape
    rows = cvec.shape[0]
    tn = 1024
    return pl.pallas_call(
        _adaln_kernel,
        grid=(depth, n6 // tn),
        in_specs=[
            pl.BlockSpec((rows, d), lambda l, j: (0, 0)),
            pl.BlockSpec((1, d, tn), lambda l, j: (l, 0, j)),
            pl.BlockSpec((1, 1, tn), lambda l, j: (l, 0, j)),
        ],
        out_specs=pl.BlockSpec((1, rows, tn), lambda l, j: (l, 0, j)),
        out_shape=jax.ShapeDtypeStruct((depth, rows, n6), F32),
        compiler_params=_params("arbitrary", "arbitrary"),
        name="adaln",
    )(cvec, ada_w, ada_b.reshape(depth, 1, n6))


def _seg(i, tiles_per_batch):
    return jnp.where(i % tiles_per_batch == 0, 0, 1 + i // tiles_per_batch)


def _mod_spec(chunk, tiles_per_batch):
    return pl.BlockSpec((1, 1, D_MODEL), lambda i: (_seg(i, tiles_per_batch) * 6 + chunk, 0, 0))


def _inproj_kernel(x_ref, shift_ref, scale_ref, w_ref, lru_ref, mla_ref):
    h = _rms(x_ref[...]) * (1.0 + scale_ref[0]) + shift_ref[0]
    y = _dot(h.astype(BF16), w_ref[...])
    lru_ref[...] = y[:, :2 * LRU_WIDTH]
    mla_ref[...] = y[:, 2 * LRU_WIDTH:]


def _inproj(x, mod, w_in_p, tpb):
    t = x.shape[0]
    n_out = w_in_p.shape[1]
    return pl.pallas_call(
        _inproj_kernel,
        grid=(t // TM,),
        in_specs=[
            pl.BlockSpec((TM, D_MODEL), lambda i: (i, 0)),
            _mod_spec(0, tpb),
            _mod_spec(1, tpb),
            _resident((D_MODEL, n_out)),
        ],
        out_specs=[
            pl.BlockSpec((TM, 2 * LRU_WIDTH), lambda i: (i, 0)),
            pl.BlockSpec((TM, MLA_COLS), lambda i: (i, 0)),
        ],
        out_shape=[
            jax.ShapeDtypeStruct((t, 2 * LRU_WIDTH), F32),
            jax.ShapeDtypeStruct((t, MLA_COLS), F32),
        ],
        compiler_params=_params("arbitrary"),
        name="inproj",
    )(x, mod, mod, w_in_p)


def _swap_halves(x):
    lane = lax.broadcasted_iota(jnp.int32, x.shape, 1)
    up = pltpu.roll(x, LANES - 16, 1)
    down = pltpu.roll(x, 16, 1)
    return jnp.where((lane & 31) < 16, up, down)


def _mla_prep_kernel(p_ref, qa_ref, kva_ref, qn_ref, kn_ref, cos_ref, sin_ref, wuq_ref, wukv_ref,
                     q_ref, k_ref, vt_ref):
    cos = cos_ref[...]
    sin = sin_ref[...]
    scale = QK_HEAD ** -0.5 * LOG2_E

    def rope(x):
        return x * cos + _swap_halves(x) * sin

    cq = p_ref[:, :Q_RANK]
    qf = _dot((_rms(cq) * qa_ref[...]).astype(BF16), wuq_ref[...])
    ckv = p_ref[:, Q_RANK:Q_RANK + KV_RANK]
    kvf = _dot((_rms(ckv) * kva_ref[...]).astype(BF16), wukv_ref[...])
    kr = p_ref[:, Q_RANK + KV_RANK:]
    k_rope = rope(_rms(kr, QK_ROPE) * kn_ref[:, QK_NOPE:]).astype(BF16)
    for h in range(MLA_HEADS):
        c0 = h * QK_PAD
        q_nope = _rms(qf[:, c0:c0 + QK_NOPE]) * qn_ref[:, :QK_NOPE]
        q_rope = rope(_rms(qf[:, c0 + QK_NOPE:c0 + QK_PAD], QK_ROPE) * qn_ref[:, QK_NOPE:])
        q_ref[h, :, :QK_NOPE] = (q_nope * scale).astype(BF16)
        q_ref[h, :, QK_NOPE:] = (q_rope * scale).astype(BF16)
        k_nope = _rms(kvf[:, h * QK_NOPE:(h + 1) * QK_NOPE]) * kn_ref[:, :QK_NOPE]
        k_ref[h, :, :QK_NOPE] = k_nope.astype(BF16)
        k_ref[h, :, QK_NOPE:] = k_rope
        v0 = MLA_HEADS * QK_NOPE + h * V_HEAD
        vt_ref[h] = kvf[:, v0:v0 + V_HEAD].T.astype(BF16)


def _mla_prep(mla_in, qa, kva, qn, kn, cos_t, sin_t, wuq_p, wukv_p, tpb):
    t = mla_in.shape[0]
    row = lambda n: pl.BlockSpec((1, n), lambda i: (0, 0))
    return pl.pallas_call(
        _mla_prep_kernel,
        grid=(t // TM,),
        in_specs=[
            pl.BlockSpec((TM, MLA_COLS), lambda i: (i, 0)),
            row(Q_RANK), row(KV_RANK), row(QK_PAD), row(QK_PAD),
            pl.BlockSpec((TM, LANES), lambda i: (i % tpb, 0)),
            pl.BlockSpec((TM, LANES), lambda i: (i % tpb, 0)),
            _resident(wuq_p.shape),
            _resident(wukv_p.shape),
        ],
        out_specs=[
            pl.BlockSpec((MLA_HEADS, TM, QK_PAD), lambda i: (0, i, 0)),
            pl.BlockSpec((MLA_HEADS, TM, QK_PAD), lambda i: (0, i, 0)),
            pl.BlockSpec((MLA_HEADS, V_HEAD, TM), lambda i: (0, 0, i)),
        ],
        out_shape=[
            jax.ShapeDtypeStruct((MLA_HEADS, t, QK_PAD), BF16),
            jax.ShapeDtypeStruct((MLA_HEADS, t, QK_PAD), BF16),
            jax.ShapeDtypeStruct((MLA_HEADS, V_HEAD, t), BF16),
        ],
        compiler_params=_params("arbitrary"),
        name="mla_prep",
    )(mla_in, qa, kva, qn, kn, cos_t, sin_t, wuq_p, wukv_p)


def _column_reduce(x, op):
    return op(x.reshape(x.shape[0] // SUBLANES, SUBLANES, x.shape[1]), axis=0)


def _attn_kernel(q_ref, k_ref, vt_ref, o_ref, s0_ref, s1_ref, *, latent_chunks):
    is_ctx = pl.program_id(1) == 0
    ctx_only = ((0, TM),)
    every_key = ctx_only + latent_chunks

    def score_pass(j, s_ref, chunks):
        q = q_ref[j]
        m8 = None
        for start, size in chunks:
            s = lax.dot_general(k_ref[j, start:start + size, :], q, _NT, preferred_element_type=F32)
            s_ref[start:start + size, :] = s
            cm = _column_reduce(s, jnp.max)
            m8 = cm if m8 is None else jnp.maximum(m8, cm)
        return jnp.max(m8, axis=0, keepdims=True)

    def value_pass(j, m, s_ref, chunks):
        l8 = jnp.zeros((SUBLANES, TM), F32)
        acc = jnp.zeros((V_HEAD, TM), F32)
        for start, size in chunks:
            e = jnp.exp2(s_ref[start:start + size, :] - m)
            l8 = l8 + _column_reduce(e, jnp.sum)
            acc = acc + _dot(vt_ref[j, :, start:start + size], e.astype(BF16))
        l = jnp.sum(l8, axis=0, keepdims=True)
        o_ref[j] = (acc / l).T

    def all_heads(chunks):
        def pair(i, m_even):
            h = 2 * i
            m_odd = score_pass(h + 1, s1_ref, chunks)
            value_pass(h, m_even, s0_ref, chunks)
            m_even = score_pass(h + 2, s0_ref, chunks)
            value_pass(h + 1, m_odd, s1_ref, chunks)
            return m_even

        m_even = lax.fori_loop(0, MLA_HEADS // 2 - 1, pair, score_pass(0, s0_ref, chunks))
        m_odd = score_pass(MLA_HEADS - 1, s1_ref, chunks)
        value_pass(MLA_HEADS - 2, m_even, s0_ref, chunks)
        value_pass(MLA_HEADS - 1, m_odd, s1_ref, chunks)

    @pl.when(is_ctx)
    def _():
        all_heads(ctx_only)

    @pl.when(jnp.logical_not(is_ctx))
    def _():
        all_heads(every_key)


def _attention(q, k, vt, batch, tpb):
    t = q.shape[1]
    rows = tpb * TM
    n_chunks = (rows - TM) // ATTN_KV_CHUNK
    latent_chunks = tuple((TM + c * ATTN_KV_CHUNK, ATTN_KV_CHUNK) for c in range(n_chunks))
    once = pl.Buffered(1)
    return pl.pallas_call(
        functools.partial(_attn_kernel, latent_chunks=latent_chunks),
        grid=(batch, tpb),
        in_specs=[
            pl.BlockSpec((MLA_HEADS, TM, QK_PAD), lambda b, i: (0, b * tpb + i, 0)),
            pl.BlockSpec((MLA_HEADS, rows, QK_PAD), lambda b, i: (0, b, 0), pipeline_mode=once),
            pl.BlockSpec((MLA_HEADS, V_HEAD, rows), lambda b, i: (0, 0, b), pipeline_mode=once),
        ],
        out_specs=pl.BlockSpec((MLA_HEADS, TM, V_HEAD), lambda b, i: (0, b * tpb + i, 0)),
        out_shape=jax.ShapeDtypeStruct((MLA_HEADS, t, V_HEAD), F32),
        scratch_shapes=[pltpu.VMEM((rows, TM), F32), pltpu.VMEM((rows, TM), F32)],
        compiler_params=_params("arbitrary", "arbitrary"),
        name="attention",
    )(q, k, vt)


def _shift_rows(x, s, fill, reverse):
    rows = x.shape[0]
    pad = jnp.full((s, x.shape[1]), fill, x.dtype)
    if s % SUBLANES == 0:
        return jnp.concatenate([x[s:], pad], 0) if reverse else jnp.concatenate([pad, x[:rows - s]], 0)
    row = lax.broadcasted_iota(jnp.int32, x.shape, 0)
    if reverse:
        return jnp.where(row < rows - s, pltpu.roll(x, rows - s, 0), fill)
    return jnp.where(row >= s, pltpu.roll(x, s, 0), fill)


def _lru_kernel(*refs, reverse, tpb):
    if reverse:
        (x_ref, prev_ref, next_ref, cw_ref, cb_ref, w_ref, ba_ref, bx_ref, lam_ref, hf_ref, xg_ref,
         o_ref, carry_ref) = refs
    else:
        (x_ref, prev_ref, next_ref, cw_ref, cb_ref, w_ref, ba_ref, bx_ref, lam_ref,
         o_ref, carry_ref) = refs
    step = pl.program_id(1)
    tile = _lru_tile(step, tpb, reverse)

    @pl.when(step == 0)
    def _():
        carry_ref[...] = jnp.zeros_like(carry_ref)

    x = x_ref[...]
    row = lax.broadcasted_iota(jnp.int32, x.shape, 0)
    at_start = tile <= 1
    at_end = jnp.logical_or(tile == 0, tile == tpb - 1)
    before1 = jnp.where(at_start, 0.0, prev_ref[SUBLANES - 1:SUBLANES, :])
    before2 = jnp.where(at_start, 0.0, prev_ref[SUBLANES - 2:SUBLANES - 1, :])
    after1 = jnp.where(at_end, 0.0, next_ref[0:1, :])
    x_m1 = jnp.where(row == 0, before1, pltpu.roll(x, 1, 0))
    x_m2 = jnp.where(row == 0, before2, jnp.where(row == 1, before1, pltpu.roll(x, 2, 0)))
    x_p1 = jnp.where(row == TM - 1, after1, pltpu.roll(x, TM - 1, 0))
    u = cb_ref[...] + (((x_m2 * cw_ref[0:1, :] + x_m1 * cw_ref[1:2, :]) + x * cw_ref[2:3, :])
                       + x_p1 * cw_ref[3:4, :])

    decay = -LRU_C * jax.nn.softplus(-lam_ref[...])
    for h in range(LRU_HEADS):
        c0 = h * LRU_HEAD_DIM
        sl = slice(c0, c0 + LRU_HEAD_DIM)
        uh = u[:, sl]
        z = _dot(uh.astype(BF16), w_ref[h])
        r = jax.nn.sigmoid(z[:, :LRU_HEAD_DIM] + ba_ref[:, sl])
        i = jax.nn.sigmoid(z[:, LRU_HEAD_DIM:] + bx_ref[:, sl])
        log_a = r * decay[:, sl]
        a = jnp.exp(log_a)
        b = jnp.sqrt(-jnp.tanh(log_a) * (1.0 + a * a)) * (i * uh)
        s = 1
        while s < TM:
            b = a * _shift_rows(b, s, 0.0, reverse) + b
            a = a * _shift_rows(a, s, 1.0, reverse)
            s *= 2
        hcur = b + a * carry_ref[0:1, sl]
        edge = 0 if reverse else TM - 1
        carry_ref[0:1, sl] = hcur[edge:edge + 1, :]
        if reverse:
            o_ref[:, sl] = jax.nn.gelu(xg_ref[:, sl]) * (hf_ref[:, sl] + hcur)
        else:
            o_ref[:, sl] = hcur


def _lru_tile(step, tpb, reverse):
    if not reverse:
        return step
    return jnp.where(step == 0, 0, tpb - step)


def _lru(lru_in, conv_w, conv_b, w_gate, ba, bx, lam, batch, tpb, reverse, h_fwd=None):
    t = lru_in.shape[0]
    halo_blocks = t // SUBLANES
    per_tile = TM // SUBLANES

    def tile_idx(b, s):
        return b * tpb + _lru_tile(s, tpb, reverse)

    row = lambda n: pl.BlockSpec((1, n), lambda b, s: (0, 0))
    in_specs = [
        pl.BlockSpec((TM, LRU_WIDTH), lambda b, s: (tile_idx(b, s), 0)),
        pl.BlockSpec((SUBLANES, LRU_WIDTH), lambda b, s: (jnp.maximum(tile_idx(b, s) * per_tile - 1, 0), 0)),
        pl.BlockSpec((SUBLANES, LRU_WIDTH),
                     lambda b, s: (jnp.minimum((tile_idx(b, s) + 1) * per_tile, halo_blocks - 1), 0)),
        pl.BlockSpec((CONV_W, LRU_WIDTH), lambda b, s: (0, 0)),
        row(LRU_WIDTH),
        pl.BlockSpec((LRU_HEADS, LRU_HEAD_DIM, 2 * LRU_HEAD_DIM), lambda b, s: (0, 0, 0)),
        row(LRU_WIDTH), row(LRU_WIDTH), row(LRU_WIDTH),
    ]
    args = [lru_in, lru_in, lru_in, conv_w, conv_b, w_gate, ba, bx, lam]
    if reverse:
        in_specs += [
            pl.BlockSpec((TM, LRU_WIDTH), lambda b, s: (tile_idx(b, s), 0)),
            pl.BlockSpec((TM, LRU_WIDTH), lambda b, s: (tile_idx(b, s), 1)),
        ]
        args += [h_fwd, lru_in]
    return pl.pallas_call(
        functools.partial(_lru_kernel, reverse=reverse, tpb=tpb),
        grid=(batch, tpb),
        in_specs=in_specs,
        out_specs=pl.BlockSpec((TM, LRU_WIDTH), lambda b, s: (tile_idx(b, s), 0)),
        out_shape=jax.ShapeDtypeStruct((t, LRU_WIDTH), F32),
        scratch_shapes=[pltpu.VMEM((SUBLANES, LRU_WIDTH), F32)],
        compiler_params=_params("arbitrary", "arbitrary"),
        name="lru_rev" if reverse else "lru_fwd",
    )(*args)


def _first_index(mask, lane_f):
    return jnp.min(jnp.where(mask, lane_f, float(ROUTE_LANES)), axis=-1, keepdims=True)


def _route(logits, rb):
    lane = lax.broadcasted_iota(jnp.int32, logits.shape, 1)
    lane_f = lane.astype(F32)
    group = lane >> 3
    scores = jax.nn.sigmoid(logits)
    biased = scores + rb
    neg = -jnp.inf
    best = e1 = e2 = None
    for g in range(N_GROUPS):
        vg = jnp.where(group == g, biased, neg)
        m1 = jnp.max(vg, axis=-1, keepdims=True)
        i1 = _first_index(vg == m1, lane_f)
        vg2 = jnp.where(lane_f == i1, neg, vg)
        m2 = jnp.max(vg2, axis=-1, keepdims=True)
        i2 = _first_index(vg2 == m2, lane_f)
        gs = m1 + m2
        if g == 0:
            best, e1, e2 = gs, i1, i2
        else:
            better = gs > best
            best = jnp.where(better, gs, best)
            e1 = jnp.where(better, i1, e1)
            e2 = jnp.where(better, i2, e2)
    s1 = jnp.sum(jnp.where(lane_f == e1, scores, 0.0), axis=-1, keepdims=True)
    s2 = jnp.sum(jnp.where(lane_f == e2, scores, 0.0), axis=-1, keepdims=True)
    tot = s1 + s2
    out = jnp.where(lane == 0, e1, jnp.where(lane == 1, e2, jnp.where(lane == 2, s1 / tot, s2 / tot)))
    return jnp.where(lane < 4, out, 0.0)


def _pack_rows(h_bf16):
    bits = lax.bitcast_convert_type(h_bf16.astype(F32), U32)
    return bits[:, :PACKED] | (bits[:, PACKED:] >> 16)


def _unpack_rows(words):
    hi = lax.bitcast_convert_type(words & jnp.uint32(0xFFFF0000), F32)
    lo = lax.bitcast_convert_type(words << 16, F32)
    return jnp.concatenate([hi, lo], axis=1).astype(BF16)


def _merge_kernel(attn_ref, rec_ref, x_ref, on_ref, gate_ref, shift_ref, scale_ref, wout_ref,
                  rwh_ref, rwl_ref, rb_ref, xn_ref, h_ref, route_ref):
    attn = jnp.concatenate([attn_ref[h] for h in range(MLA_HEADS)], axis=1)
    ya = (_rms(attn) * on_ref[:, :MLA_WIDTH]).astype(BF16)
    yr = (_rms(rec_ref[...]) * on_ref[:, MLA_WIDTH:]).astype(BF16)
    y = _dot(ya, wout_ref[:MLA_WIDTH, :]) + _dot(yr, wout_ref[MLA_WIDTH:, :])
    xn = x_ref[...] + gate_ref[0] * y
    xn_ref[...] = xn
    h = _rms(xn) * (1.0 + scale_ref[0]) + shift_ref[0]
    h_hi = h.astype(BF16)
    h_ref[...] = _pack_rows(h_hi)
    h_lo = (h - h_hi.astype(F32)).astype(BF16)
    logits = (_dot(h_hi, rwh_ref[...]) + _dot(h_lo, rwh_ref[...])) + _dot(h_hi, rwl_ref[...])
    route_ref[...] = _route(logits, rb_ref[...])


def _merge(attn, rec, x, mod, out_norm, w_out, rw_hi, rw_lo, rb, tpb):
    t = x.shape[0]
    tile = lambda n: pl.BlockSpec((TM, n), lambda i: (i, 0))
    return pl.pallas_call(
        _merge_kernel,
        grid=(t // TM,),
        in_specs=[
            pl.BlockSpec((MLA_HEADS, TM, V_HEAD), lambda i: (0, i, 0)), tile(LRU_WIDTH), tile(D_MODEL),
            pl.BlockSpec((1, D_MODEL), lambda i: (0, 0)),
            _mod_spec(2, tpb), _mod_spec(3, tpb), _mod_spec(4, tpb),
            _resident(w_out.shape), _resident(rw_hi.shape), _resident(rw_lo.shape),
            pl.BlockSpec((1, ROUTE_LANES), lambda i: (0, 0)),
        ],
        out_specs=[tile(D_MODEL), tile(PACKED), tile(ROUTE_LANES)],
        out_shape=[
            jax.ShapeDtypeStruct((t, D_MODEL), F32),
            jax.ShapeDtypeStruct((t, PACKED), U32),
            jax.ShapeDtypeStruct((t, ROUTE_LANES), F32),
        ],
        compiler_params=_params("arbitrary"),
        name="merge",
    )(attn, rec, x, out_norm, mod, mod, mod, w_out, rw_hi, rw_lo, rb)


def _gather_rows(idx_ref, n, src_ref, dst_ref, sem):
    def copy(r):
        return pltpu.make_async_copy(src_ref.at[pl.ds(idx_ref[0, 0, r], 1)], dst_ref.at[pl.ds(r, 1)], sem)

    def start(r, c):
        copy(r).start()
        return c

    def wait(r, c):
        copy(r).wait()
        return c

    lax.fori_loop(0, n, start, 0, unroll=DMA_UNROLL)
    lax.fori_loop(0, n, wait, 0, unroll=DMA_UNROLL)


def _moe_gather_kernel(used_ref, idx_ref, src_ref, o_ref, sem):
    j = pl.program_id(0)

    @pl.when(j < used_ref[0])
    def _():
        _gather_rows(idx_ref, MOE_TILE, src_ref, o_ref, sem)

    @pl.when(j >= used_ref[0])
    def _():
        o_ref[...] = jnp.zeros_like(o_ref)


def _moe_gather(h_packed, tok, n_used):
    n_tiles = tok.shape[0]
    grid_spec = pltpu.PrefetchScalarGridSpec(
        num_scalar_prefetch=1,
        grid=(n_tiles,),
        in_specs=[
            pl.BlockSpec((1, 1, MOE_TILE), lambda j, used: (j, 0, 0), memory_space=pltpu.SMEM),
            pl.BlockSpec(memory_space=pl.ANY),
        ],
        out_specs=pl.BlockSpec((MOE_TILE, PACKED), lambda j, used: (j, 0)),
        scratch_shapes=[pltpu.SemaphoreType.DMA(())],
    )
    return pl.pallas_call(
        _moe_gather_kernel,
        grid_spec=grid_spec,
        out_shape=jax.ShapeDtypeStruct((n_tiles * MOE_TILE, PACKED), U32),
        compiler_params=_params("arbitrary"),
        name="moe_gather",
    )(n_used, tok, h_packed)


def _expert_kernel(used_ref, te_ref, first_ref, nxt_ref, par_ref, x_ref, wg_hbm, wu_hbm, wd_hbm, o_ref,
                   wg_buf, wu_buf, wd_buf, sem, *, layer):
    j = pl.program_id(0)

    def fetch(expert, slot):
        return (pltpu.make_async_copy(wg_hbm.at[layer, expert], wg_buf.at[slot], sem.at[slot, 0]),
                pltpu.make_async_copy(wu_hbm.at[layer, expert], wu_buf.at[slot], sem.at[slot, 1]),
                pltpu.make_async_copy(wd_hbm.at[layer, expert], wd_buf.at[slot], sem.at[slot, 2]))

    @pl.when(j < used_ref[0])
    def _():
        slot = par_ref[j]

        @pl.when(j == 0)
        def _():
            for c in fetch(te_ref[0], 0):
                c.start()

        @pl.when(first_ref[j] == 1)
        def _():
            for c in fetch(te_ref[j], slot):
                c.wait()

            @pl.when(nxt_ref[j] >= 0)
            def _():
                for c in fetch(nxt_ref[j], 1 - slot):
                    c.start()

        x = _unpack_rows(x_ref[...])
        acc = jnp.zeros((MOE_TILE, D_MODEL), F32)
        for c in range(D_FF_EXPERT // MOE_FF_CHUNK):
            cs = slice(c * MOE_FF_CHUNK, (c + 1) * MOE_FF_CHUNK)
            g = _dot(x, wg_buf[slot, :, cs].astype(BF16))
            u = _dot(x, wu_buf[slot, :, cs].astype(BF16))
            act = (jax.nn.silu(g) * u).astype(BF16)
            acc = acc + _dot(act, wd_buf[slot, cs, :].astype(BF16))
        o_ref[...] = acc

    @pl.when(j >= used_ref[0])
    def _():
        o_ref[...] = jnp.zeros_like(o_ref)


def _experts(x_sorted, plan, wg, wu, wd, layer):
    rows = x_sorted.shape[0]
    n_tiles = rows // MOE_TILE
    last_used = lambda j, used, *_: (jnp.minimum(j, used[0] - 1), 0)
    grid_spec = pltpu.PrefetchScalarGridSpec(
        num_scalar_prefetch=5,
        grid=(n_tiles,),
        in_specs=[
            pl.BlockSpec((MOE_TILE, PACKED), last_used),
            pl.BlockSpec(memory_space=pl.ANY),
            pl.BlockSpec(memory_space=pl.ANY),
            pl.BlockSpec(memory_space=pl.ANY),
        ],
        out_specs=pl.BlockSpec((MOE_TILE, D_MODEL), lambda j, *_: (j, 0)),
        scratch_shapes=[
            pltpu.VMEM((2, D_MODEL, D_FF_EXPERT), F32),
            pltpu.VMEM((2, D_MODEL, D_FF_EXPERT), F32),
            pltpu.VMEM((2, D_FF_EXPERT, D_MODEL), F32),
            pltpu.SemaphoreType.DMA((2, 3)),
        ],
    )
    return pl.pallas_call(
        functools.partial(_expert_kernel, layer=layer),
        grid_spec=grid_spec,
        out_shape=jax.ShapeDtypeStruct((rows, D_MODEL), F32),
        compiler_params=_params("arbitrary"),
        name="experts",
    )(plan['n_used'], plan['te'], plan['first'], plan['nxt'], plan['par'], x_sorted, wg, wu, wd)


def _combine_kernel(idx_ref, route_ref, xn_ref, gate_ref, y_hbm, o_ref, buf, sem):
    _gather_rows(idx_ref, 2 * TM, y_hbm, buf, sem)
    y = buf[:TM, :] * route_ref[:, 2:3] + buf[TM:, :] * route_ref[:, 3:4]
    o_ref[...] = xn_ref[...] + gate_ref[0] * y


def _combine(slots, route, xn, mod, y_sorted, tpb):
    t = xn.shape[0]
    return pl.pallas_call(
        _combine_kernel,
        grid=(t // TM,),
        in_specs=[
            pl.BlockSpec((1, 1, 2 * TM), lambda i: (i, 0, 0), memory_space=pltpu.SMEM),
            pl.BlockSpec((TM, ROUTE_LANES), lambda i: (i, 0)),
            pl.BlockSpec((TM, D_MODEL), lambda i: (i, 0)),
            _mod_spec(5, tpb),
            pl.BlockSpec(memory_space=pl.ANY),
        ],
        out_specs=pl.BlockSpec((TM, D_MODEL), lambda i: (i, 0)),
        out_shape=jax.ShapeDtypeStruct((t, D_MODEL), F32),
        scratch_shapes=[pltpu.VMEM((2 * TM, D_MODEL), F32), pltpu.SemaphoreType.DMA(())],
        compiler_params=_params("arbitrary"),
        name="combine",
    )(slots, route, xn, mod, y_sorted)


def _dispatch(route):
    t = route.shape[0]
    flat_e = route[:, :2].astype(jnp.int32).reshape(-1)
    n_pairs = flat_e.shape[0]
    experts = jnp.arange(N_EXPERTS, dtype=jnp.int32)
    onehot = (flat_e[:, None] == experts[None, :]).astype(jnp.int32)
    csum = jnp.cumsum(onehot, axis=0)
    rank = jnp.sum(csum * onehot, axis=1) - 1
    counts = csum[-1]
    padded = (counts + MOE_TILE - 1) // MOE_TILE * MOE_TILE
    pend = jnp.cumsum(padded)
    slot = jnp.sum((pend - padded)[None, :] * onehot, axis=1) + rank
    n_tiles = -(-n_pairs // MOE_TILE) + N_EXPERTS
    n_used = (pend[-1] // MOE_TILE).astype(jnp.int32)
    tile_ids = jnp.arange(n_tiles, dtype=jnp.int32)
    used = tile_ids < n_used
    te = jnp.minimum(jnp.sum((pend[None, :] <= (tile_ids * MOE_TILE)[:, None]).astype(jnp.int32), axis=1),
                     N_EXPERTS - 1)
    prev = jnp.concatenate([jnp.full((1,), -1, jnp.int32), te[:-1]])
    first = jnp.logical_and(used, te != prev).astype(jnp.int32)
    par = (jnp.cumsum(first) - 1) % 2
    later = jnp.logical_and(experts[None, :] > experts[:, None], (counts > 0)[None, :])
    next_expert = jnp.min(jnp.where(later, experts[None, :], N_EXPERTS), axis=1)
    next_expert = jnp.where(next_expert == N_EXPERTS, -1, next_expert)
    tok = jnp.zeros((n_tiles * MOE_TILE,), jnp.int32).at[slot].set(jnp.arange(n_pairs, dtype=jnp.int32) // 2)
    slots = slot.reshape(t // TM, TM, 2).transpose(0, 2, 1).reshape(t // TM, 1, 2 * TM)
    plan = dict(n_used=n_used.reshape(1), te=te.astype(jnp.int32), first=first,
                nxt=next_expert[te].astype(jnp.int32), par=par.astype(jnp.int32))
    return slots, tok.reshape(n_tiles, 1, MOE_TILE), plan


def _rope_tables(n_latent):
    rows = n_latent // GRID_W
    row = jnp.repeat(jnp.arange(rows), GRID_W).astype(F32)
    col = jnp.tile(jnp.arange(GRID_W), rows).astype(F32)
    n_freq = QK_ROPE // 4
    inv_freq = ROPE_BASE ** (-jnp.arange(n_freq, dtype=F32) / n_freq)
    ar, ac = row[:, None] * inv_freq[None, :], col[:, None] * inv_freq[None, :]
    zeros = jnp.zeros((n_latent, LANES - QK_ROPE), F32)
    cos = jnp.concatenate([jnp.cos(ar), jnp.cos(ar), jnp.cos(ac), jnp.cos(ac), zeros], axis=1)
    sin = jnp.concatenate([-jnp.sin(ar), jnp.sin(ar), -jnp.sin(ac), jnp.sin(ac), zeros], axis=1)
    ctx_cos = jnp.concatenate([jnp.ones((TM, QK_ROPE), F32), jnp.zeros((TM, LANES - QK_ROPE), F32)], axis=1)
    return (jnp.concatenate([ctx_cos, cos], axis=0),
            jnp.concatenate([jnp.zeros((TM, LANES), F32), sin], axis=0))


def _pad_cols(w, n):
    return jnp.concatenate([w, jnp.zeros(w.shape[:-1] + (n,), w.dtype)], axis=-1)


def _layer_params(l, w_in, q_a_norm, w_uq, kv_a_norm, w_ukv, q_norm, k_norm, conv_w, conv_b, lru_wa, lru_ba,
                  lru_wx, lru_bx, lru_lambda, out_norm, w_out):
    s0, s1, s2, s3 = Q_RANK, Q_RANK + KV_RANK, Q_RANK + KV_RANK + QK_ROPE, Q_RANK + KV_RANK + QK_ROPE + LRU_WIDTH
    w = w_in[l]
    w_in_p = jnp.concatenate([w[:, s2:s3], w[:, s3:], w[:, :s0], w[:, s0:s1],
                              _pad_cols(w[:, s1:s2], LANES - QK_ROPE)], axis=1).astype(BF16)
    wq = w_uq[l].reshape(Q_RANK, MLA_HEADS, QK_HEAD)
    wuq_p = _pad_cols(wq, QK_PAD - QK_HEAD).reshape(Q_RANK, MLA_HEADS * QK_PAD).astype(BF16)
    wkv = w_ukv[l].reshape(KV_RANK, MLA_HEADS, QK_NOPE + V_HEAD)
    wukv_p = jnp.concatenate([wkv[:, :, :QK_NOPE].reshape(KV_RANK, -1),
                              wkv[:, :, QK_NOPE:].reshape(KV_RANK, -1)], axis=1).astype(BF16)
    w_gate = [jnp.concatenate([lru_wa[l, d], lru_wx[l, d]], axis=-1).astype(BF16) for d in range(2)]
    return dict(
        w_in=w_in_p, wuq=wuq_p, wukv=wukv_p,
        qa=q_a_norm[l][None], kva=kv_a_norm[l][None],
        qn=_pad_cols(q_norm[l], QK_PAD - QK_HEAD)[None], kn=_pad_cols(k_norm[l], QK_PAD - QK_HEAD)[None],
        conv_w=conv_w[l], conv_b=conv_b[l][None], w_gate=w_gate,
        ba=[lru_ba[l, d][None] for d in range(2)], bx=[lru_bx[l, d][None] for d in range(2)],
        lam=[lru_lambda[l, d][None] for d in range(2)],
        out_norm=out_norm[l][None], w_out=w_out[l].astype(BF16),
    )


def kernel(x, c, ctx, c_ctx, ada_w, ada_b, w_in, q_a_norm, w_uq, kv_a_norm, w_ukv, q_norm, k_norm, conv_w, conv_b, lru_wa, lru_ba, lru_wx, lru_bx, lru_lambda, out_norm, w_out, router_w, router_b, moe_w_gate, moe_w_up, moe_w_down):
    batch, n, d = x.shape
    depth = ada_w.shape[0]
    assert d == D_MODEL and ctx.shape[1] == TM and n % ATTN_KV_CHUNK == 0 and n % GRID_W == 0
    assert 1 + batch <= SUBLANES
    tpb = (TM + n) // TM
    t = batch * tpb * TM

    cvec = jnp.concatenate([c_ctx[None], c, jnp.zeros((SUBLANES - 1 - batch, d), F32)], axis=0)
    mod_all = _adaln(cvec, ada_w, ada_b)
    cos_t, sin_t = _rope_tables(n)
    rw = _pad_cols(router_w, ROUTE_LANES - N_EXPERTS)
    rw_hi = rw.astype(BF16)
    rw_lo = (rw - rw_hi.astype(F32)).astype(BF16)
    rb = _pad_cols(router_b, ROUTE_LANES - N_EXPERTS)[None]

    xs = jnp.concatenate([ctx, x], axis=1).reshape(t, d)
    for l in range(depth):
        p = _layer_params(l, w_in, q_a_norm, w_uq, kv_a_norm, w_ukv, q_norm, k_norm, conv_w, conv_b, lru_wa,
                          lru_ba, lru_wx, lru_bx, lru_lambda, out_norm, w_out)
        mod = mod_all[l].reshape(SUBLANES * 6, 1, d)
        lru_in, mla_in = _inproj(xs, mod, p['w_in'], tpb)
        q, k, vt = _mla_prep(mla_in, p['qa'], p['kva'], p['qn'], p['kn'], cos_t, sin_t, p['wuq'], p['wukv'], tpb)
        attn = _attention(q, k, vt, batch, tpb)
        h_fwd = _lru(lru_in, p['conv_w'], p['conv_b'], p['w_gate'][0], p['ba'][0], p['bx'][0], p['lam'][0],
                     batch, tpb, False)
        rec = _lru(lru_in, p['conv_w'], p['conv_b'], p['w_gate'][1], p['ba'][1], p['bx'][1], p['lam'][1],
                   batch, tpb, True, h_fwd)
        xn, h_packed, route = _merge(attn, rec, xs, mod, p['out_norm'], p['w_out'], rw_hi, rw_lo, rb, tpb)
        slots, tok, plan = _dispatch(route)
        x_sorted = _moe_gather(h_packed, tok, plan['n_used'])
        y_sorted = _experts(x_sorted, plan, moe_w_gate, moe_w_up, moe_w_down, l)
        xs = _combine(slots, route, xn, mod, y_sorted, tpb)
    return xs.reshape(batch, tpb * TM, d)[:, TM:]
```

```python
import functools

import jax
import jax.numpy as jnp
from jax import lax
from jax.experimental import pallas as pl
from jax.experimental.pallas import tpu as pltpu

F32 = jnp.float32
BF16 = jnp.bfloat16

D_MODEL = 2048
MLA_HEADS = 8
QK_NOPE = 128
QK_ROPE = 64
QK_HEAD = QK_NOPE + QK_ROPE
V_HEAD = 128
Q_RANK = 512
KV_RANK = 256
MLA_WIDTH = MLA_HEADS * V_HEAD
LRU_WIDTH = D_MODEL - MLA_WIDTH
LRU_HEADS = 8
LRU_HEAD_DIM = LRU_WIDTH // LRU_HEADS
CONV_W = 4
LRU_C = 8.0
N_EXPERTS = 32
N_GROUPS = 4
EXPERTS_PER_GROUP = N_EXPERTS // N_GROUPS
D_FF_EXPERT = 768
GRID_W = 64
ROPE_BASE = 10000.0
EPS = 1e-6
LOG2_E = 1.4426950408889634

LANES = 128
SUBLANES = 8
VMEM_LIMIT = 56 * 1024 * 1024

TM = 256
QK_PAD = 2 * LANES
MLA_COLS = Q_RANK + KV_RANK + LANES
ATTN_KV_CHUNK = 512
MOE_TILE = 256
MOE_FF_CHUNK = 256
ROUTE_LANES = LANES
DMA_UNROLL = 8
WEIGHT_DMA_PRIORITY = 1

_NT = (((1,), (1,)), ((), ()))


def _dot(a, b):
    return jnp.dot(a, b, preferred_element_type=F32)


def _rms(x, denom=None):
    d = x.shape[-1] if denom is None else denom
    ms = jnp.sum(x * x, axis=-1, keepdims=True) * (1.0 / d)
    return x * lax.rsqrt(ms + EPS)


def _params(*sem):
    return pltpu.CompilerParams(dimension_semantics=sem, vmem_limit_bytes=VMEM_LIMIT)


def _resident(shape):
    nd = len(shape)
    return pl.BlockSpec(shape, lambda *_: (0,) * nd, pipeline_mode=pl.Buffered(1))


def _adaln_kernel(c_ref, w_ref, b_ref, o_ref):
    s = jax.nn.silu(c_ref[...]).astype(BF16)
    o_ref[0] = _dot(s, w_ref[0].astype(BF16)) + b_ref[0]


def _adaln(cvec, ada_w, ada_b):
    depth, d, n6 = ada_w.shape
    rows = cvec.shape[0]
    tn = 1024
    return pl.pallas_call(
        _adaln_kernel,
        grid=(depth, n6 // tn),
        in_specs=[
            pl.BlockSpec((rows, d), lambda l, j: (0, 0)),
            pl.BlockSpec((1, d, tn), lambda l, j: (l, 0, j)),
            pl.BlockSpec((1, 1, tn), lambda l, j: (l, 0, j)),
        ],
        out_specs=pl.BlockSpec((1, rows, tn), lambda l, j: (l, 0, j)),
        out_shape=jax.ShapeDtypeStruct((depth, rows, n6), F32),
        compiler_params=_params("arbitrary", "arbitrary"),
        name="adaln",
    )(cvec, ada_w, ada_b.reshape(depth, 1, n6))


def _seg(i, tiles_per_batch):
    return jnp.where(i % tiles_per_batch == 0, 0, 1 + i // tiles_per_batch)


def _mod_spec(chunk, tiles_per_batch):
    return pl.BlockSpec((1, 1, D_MODEL), lambda i: (_seg(i, tiles_per_batch) * 6 + chunk, 0, 0))


def _inproj_kernel(x_ref, shift_ref, scale_ref, w_ref, lru_ref, mla_ref):
    h = _rms(x_ref[...]) * (1.0 + scale_ref[0]) + shift_ref[0]
    y = _dot(h.astype(BF16), w_ref[...])
    lru_ref[...] = y[:, :2 * LRU_WIDTH]
    mla_ref[...] = y[:, 2 * LRU_WIDTH:]


def _inproj(x, mod, w_in_p, tpb):
    t = x.shape[0]
    n_out = w_in_p.shape[1]
    return pl.pallas_call(
        _inproj_kernel,
        grid=(t // TM,),
        in_specs=[
            pl.BlockSpec((TM, D_MODEL), lambda i: (i, 0)),
            _mod_spec(0, tpb),
            _mod_spec(1, tpb),
            _resident((D_MODEL, n_out)),
        ],
        out_specs=[
            pl.BlockSpec((TM, 2 * LRU_WIDTH), lambda i: (i, 0)),
            pl.BlockSpec((TM, MLA_COLS), lambda i: (i, 0)),
        ],
        out_shape=[
            jax.ShapeDtypeStruct((t, 2 * LRU_WIDTH), F32),
            jax.ShapeDtypeStruct((t, MLA_COLS), F32),
        ],
        compiler_params=_params("arbitrary"),
        name="inproj",
    )(x, mod, mod, w_in_p)


def _swap_halves(x):
    lane = lax.broadcasted_iota(jnp.int32, x.shape, 1)
    up = pltpu.roll(x, LANES - 16, 1)
    down = pltpu.roll(x, 16, 1)
    return jnp.where((lane & 31) < 16, up, down)


def _mla_prep_kernel(p_ref, qa_ref, kva_ref, qn_ref, kn_ref, cos_ref, sin_ref, wuq_ref, wukv_ref,
                     q_ref, k_ref, vt_ref):
    cos = cos_ref[...]
    sin = sin_ref[...]
    scale = QK_HEAD ** -0.5 * LOG2_E

    def rope(x):
        return x * cos + _swap_halves(x) * sin

    cq = p_ref[:, :Q_RANK]
    qf = _dot((_rms(cq) * qa_ref[...]).astype(BF16), wuq_ref[...])
    ckv = p_ref[:, Q_RANK:Q_RANK + KV_RANK]
    kvf = _dot((_rms(ckv) * kva_ref[...]).astype(BF16), wukv_ref[...])
    kr = p_ref[:, Q_RANK + KV_RANK:]
    k_rope = rope(_rms(kr, QK_ROPE) * kn_ref[:, QK_NOPE:]).astype(BF16)
    for h in range(MLA_HEADS):
        c0 = h * QK_PAD
        q_nope = _rms(qf[:, c0:c0 + QK_NOPE]) * qn_ref[:, :QK_NOPE]
        q_rope = rope(_rms(qf[:, c0 + QK_NOPE:c0 + QK_PAD], QK_ROPE) * qn_ref[:, QK_NOPE:])
        q_ref[h, :, :QK_NOPE] = (q_nope * scale).astype(BF16)
        q_ref[h, :, QK_NOPE:] = (q_rope * scale).astype(BF16)
        k_nope = _rms(kvf[:, h * QK_NOPE:(h + 1) * QK_NOPE]) * kn_ref[:, :QK_NOPE]
        k_ref[h, :, :QK_NOPE] = k_nope.astype(BF16)
        k_ref[h, :, QK_NOPE:] = k_rope
        v0 = MLA_HEADS * QK_NOPE + h * V_HEAD
        vt_ref[h] = kvf[:, v0:v0 + V_HEAD].T.astype(BF16)


def _mla_prep(mla_in, qa, kva, qn, kn, cos_t, sin_t, wuq_p, wukv_p, tpb):
    t = mla_in.shape[0]
    row = lambda n: pl.BlockSpec((1, n), lambda i: (0, 0))
    return pl.pallas_call(
        _mla_prep_kernel,
        grid=(t // TM,),
        in_specs=[
            pl.BlockSpec((TM, MLA_COLS), lambda i: (i, 0)),
            row(Q_RANK), row(KV_RANK), row(QK_PAD), row(QK_PAD),
            pl.BlockSpec((TM, LANES), lambda i: (i % tpb, 0)),
            pl.BlockSpec((TM, LANES), lambda i: (i % tpb, 0)),
            _resident(wuq_p.shape),
            _resident(wukv_p.shape),
        ],
        out_specs=[
            pl.BlockSpec((MLA_HEADS, TM, QK_PAD), lambda i: (0, i, 0)),
            pl.BlockSpec((MLA_HEADS, TM, QK_PAD), lambda i: (0, i, 0)),
            pl.BlockSpec((MLA_HEADS, V_HEAD, TM), lambda i: (0, 0, i)),
        ],
        out_shape=[
            jax.ShapeDtypeStruct((MLA_HEADS, t, QK_PAD), BF16),
            jax.ShapeDtypeStruct((MLA_HEADS, t, QK_PAD), BF16),
            jax.ShapeDtypeStruct((MLA_HEADS, V_HEAD, t), BF16),
        ],
        compiler_params=_params("arbitrary"),
        name="mla_prep",
    )(mla_in, qa, kva, qn, kn, cos_t, sin_t, wuq_p, wukv_p)


def _column_reduce(x, op):
    return op(x.reshape(x.shape[0] // SUBLANES, SUBLANES, x.shape[1]), axis=0)


def _attn_kernel(q_ref, k_ref, vt_ref, o_ref, s0_ref, s1_ref, *, latent_chunks):
    is_ctx = pl.program_id(1) == 0
    ctx_only = ((0, TM),)
    every_key = ctx_only + latent_chunks

    def score_pass(j, s_ref, chunks):
        q = q_ref[j]
        m8 = None
        for start, size in chunks:
            s = lax.dot_general(k_ref[j, start:start + size, :], q, _NT, preferred_element_type=F32)
            s_ref[start:start + size, :] = s
            cm = _column_reduce(s, jnp.max)
            m8 = cm if m8 is None else jnp.maximum(m8, cm)
        return jnp.max(m8, axis=0, keepdims=True)

    def value_pass(j, m, s_ref, chunks):
        l8 = jnp.zeros((SUBLANES, TM), F32)
        acc = jnp.zeros((V_HEAD, TM), F32)
        for start, size in chunks:
            e = jnp.exp2(s_ref[start:start + size, :] - m)
            l8 = l8 + _column_reduce(e, jnp.sum)
            acc = acc + _dot(vt_ref[j, :, start:start + size], e.astype(BF16))
        l = jnp.sum(l8, axis=0, keepdims=True)
        o_ref[j] = (acc / l).T

    def all_heads(chunks):
        def pair(i, m_even):
            h = 2 * i
            m_odd = score_pass(h + 1, s1_ref, chunks)
            value_pass(h, m_even, s0_ref, chunks)
            m_even = score_pass(h + 2, s0_ref, chunks)
            value_pass(h + 1, m_odd, s1_ref, chunks)
            return m_even

        m_even = lax.fori_loop(0, MLA_HEADS // 2 - 1, pair, score_pass(0, s0_ref, chunks))
        m_odd = score_pass(MLA_HEADS - 1, s1_ref, chunks)
        value_pass(MLA_HEADS - 2, m_even, s0_ref, chunks)
        value_pass(MLA_HEADS - 1, m_odd, s1_ref, chunks)

    @pl.when(is_ctx)
    def _():
        all_heads(ctx_only)

    @pl.when(jnp.logical_not(is_ctx))
    def _():
        all_heads(every_key)


def _attention(q, k, vt, batch, tpb):
    t = q.shape[1]
    rows = tpb * TM
    n_chunks = (rows - TM) // ATTN_KV_CHUNK
    latent_chunks = tuple((TM + c * ATTN_KV_CHUNK, ATTN_KV_CHUNK) for c in range(n_chunks))
    once = pl.Buffered(1)
    return pl.pallas_call(
        functools.partial(_attn_kernel, latent_chunks=latent_chunks),
        grid=(batch, tpb),
        in_specs=[
            pl.BlockSpec((MLA_HEADS, TM, QK_PAD), lambda b, i: (0, b * tpb + i, 0)),
            pl.BlockSpec((MLA_HEADS, rows, QK_PAD), lambda b, i: (0, b, 0), pipeline_mode=once),
            pl.BlockSpec((MLA_HEADS, V_HEAD, rows), lambda b, i: (0, 0, b), pipeline_mode=once),
        ],
        out_specs=pl.BlockSpec((MLA_HEADS, TM, V_HEAD), lambda b, i: (0, b * tpb + i, 0)),
        out_shape=jax.ShapeDtypeStruct((MLA_HEADS, t, V_HEAD), F32),
        scratch_shapes=[pltpu.VMEM((rows, TM), F32), pltpu.VMEM((rows, TM), F32)],
        compiler_params=_params("arbitrary", "arbitrary"),
        name="attention",
    )(q, k, vt)


def _shift_rows(x, s, fill, reverse):
    rows = x.shape[0]
    pad = jnp.full((s, x.shape[1]), fill, x.dtype)
    if s % SUBLANES == 0:
        return jnp.concatenate([x[s:], pad], 0) if reverse else jnp.concatenate([pad, x[:rows - s]], 0)
    row = lax.broadcasted_iota(jnp.int32, x.shape, 0)
    if reverse:
        return jnp.where(row < rows - s, pltpu.roll(x, rows - s, 0), fill)
    return jnp.where(row >= s, pltpu.roll(x, s, 0), fill)


def _lru_kernel(*refs, reverse, tpb):
    if reverse:
        (x_ref, prev_ref, next_ref, cw_ref, cb_ref, w_ref, ba_ref, bx_ref, lam_ref, hf_ref, xg_ref,
         o_ref, carry_ref) = refs
    else:
        (x_ref, prev_ref, next_ref, cw_ref, cb_ref, w_ref, ba_ref, bx_ref, lam_ref,
         o_ref, carry_ref) = refs
    step = pl.program_id(1)
    tile = _lru_tile(step, tpb, reverse)

    @pl.when(step == 0)
    def _():
        carry_ref[...] = jnp.zeros_like(carry_ref)

    x = x_ref[...]
    row = lax.broadcasted_iota(jnp.int32, x.shape, 0)
    at_start = tile <= 1
    at_end = jnp.logical_or(tile == 0, tile == tpb - 1)
    before1 = jnp.where(at_start, 0.0, prev_ref[SUBLANES - 1:SUBLANES, :])
    before2 = jnp.where(at_start, 0.0, prev_ref[SUBLANES - 2:SUBLANES - 1, :])
    after1 = jnp.where(at_end, 0.0, next_ref[0:1, :])
    x_m1 = jnp.where(row == 0, before1, pltpu.roll(x, 1, 0))
    x_m2 = jnp.where(row == 0, before2, jnp.where(row == 1, before1, pltpu.roll(x, 2, 0)))
    x_p1 = jnp.where(row == TM - 1, after1, pltpu.roll(x, TM - 1, 0))
    u = cb_ref[...] + (((x_m2 * cw_ref[0:1, :] + x_m1 * cw_ref[1:2, :]) + x * cw_ref[2:3, :])
                       + x_p1 * cw_ref[3:4, :])

    decay = -LRU_C * jax.nn.softplus(-lam_ref[...])
    for h in range(LRU_HEADS):
        c0 = h * LRU_HEAD_DIM
        sl = slice(c0, c0 + LRU_HEAD_DIM)
        uh = u[:, sl]
        z = _dot(uh.astype(BF16), w_ref[h])
        r = jax.nn.sigmoid(z[:, :LRU_HEAD_DIM] + ba_ref[:, sl])
        i = jax.nn.sigmoid(z[:, LRU_HEAD_DIM:] + bx_ref[:, sl])
        log_a = r * decay[:, sl]
        a = jnp.exp(log_a)
        b = jnp.sqrt(-jnp.tanh(log_a) * (1.0 + a * a)) * (i * uh)
        s = 1
        while s < TM:
            b = a * _shift_rows(b, s, 0.0, reverse) + b
            a = a * _shift_rows(a, s, 1.0, reverse)
            s *= 2
        hcur = b + a * carry_ref[0:1, sl]
        edge = 0 if reverse else TM - 1
        carry_ref[0:1, sl] = hcur[edge:edge + 1, :]
        if reverse:
            o_ref[:, sl] = jax.nn.gelu(xg_ref[:, sl]) * (hf_ref[:, sl] + hcur)
        else:
            o_ref[:, sl] = hcur


def _lru_tile(step, tpb, reverse):
    if not reverse:
        return step
    return jnp.where(step == 0, 0, tpb - step)


def _lru(lru_in, conv_w, conv_b, w_gate, ba, bx, lam, batch, tpb, reverse, h_fwd=None):
    t = lru_in.shape[0]
    halo_blocks = t // SUBLANES
    per_tile = TM // SUBLANES

    def tile_idx(b, s):
        return b * tpb + _lru_tile(s, tpb, reverse)

    row = lambda n: pl.BlockSpec((1, n), lambda b, s: (0, 0))
    in_specs = [
        pl.BlockSpec((TM, LRU_WIDTH), lambda b, s: (tile_idx(b, s), 0)),
        pl.BlockSpec((SUBLANES, LRU_WIDTH), lambda b, s: (jnp.maximum(tile_idx(b, s) * per_tile - 1, 0), 0)),
        pl.BlockSpec((SUBLANES, LRU_WIDTH),
                     lambda b, s: (jnp.minimum((tile_idx(b, s) + 1) * per_tile, halo_blocks - 1), 0)),
        pl.BlockSpec((CONV_W, LRU_WIDTH), lambda b, s: (0, 0)),
        row(LRU_WIDTH),
        pl.BlockSpec((LRU_HEADS, LRU_HEAD_DIM, 2 * LRU_HEAD_DIM), lambda b, s: (0, 0, 0)),
        row(LRU_WIDTH), row(LRU_WIDTH), row(LRU_WIDTH),
    ]
    args = [lru_in, lru_in, lru_in, conv_w, conv_b, w_gate, ba, bx, lam]
    if reverse:
        in_specs += [
            pl.BlockSpec((TM, LRU_WIDTH), lambda b, s: (tile_idx(b, s), 0)),
            pl.BlockSpec((TM, LRU_WIDTH), lambda b, s: (tile_idx(b, s), 1)),
        ]
        args += [h_fwd, lru_in]
    return pl.pallas_call(
        functools.partial(_lru_kernel, reverse=reverse, tpb=tpb),
        grid=(batch, tpb),
        in_specs=in_specs,
        out_specs=pl.BlockSpec((TM, LRU_WIDTH), lambda b, s: (tile_idx(b, s), 0)),
        out_shape=jax.ShapeDtypeStruct((t, LRU_WIDTH), F32),
        scratch_shapes=[pltpu.VMEM((SUBLANES, LRU_WIDTH), F32)],
        compiler_params=_params("arbitrary", "arbitrary"),
        name="lru_rev" if reverse else "lru_fwd",
    )(*args)


def _first_index(mask, lane_f):
    return jnp.min(jnp.where(mask, lane_f, float(ROUTE_LANES)), axis=-1, keepdims=True)


def _route(logits, rb):
    lane = lax.broadcasted_iota(jnp.int32, logits.shape, 1)
    lane_f = lane.astype(F32)
    group = lane >> 3
    scores = jax.nn.sigmoid(logits)
    biased = scores + rb
    neg = -jnp.inf
    best = e1 = e2 = None
    for g in range(N_GROUPS):
        vg = jnp.where(group == g, biased, neg)
        m1 = jnp.max(vg, axis=-1, keepdims=True)
        i1 = _first_index(vg == m1, lane_f)
        vg2 = jnp.where(lane_f == i1, neg, vg)
        m2 = jnp.max(vg2, axis=-1, keepdims=True)
        i2 = _first_index(vg2 == m2, lane_f)
        gs = m1 + m2
        if g == 0:
            best, e1, e2 = gs, i1, i2
        else:
            better = gs > best
            best = jnp.where(better, gs, best)
            e1 = jnp.where(better, i1, e1)
            e2 = jnp.where(better, i2, e2)
    s1 = jnp.sum(jnp.where(lane_f == e1, scores, 0.0), axis=-1, keepdims=True)
    s2 = jnp.sum(jnp.where(lane_f == e2, scores, 0.0), axis=-1, keepdims=True)
    tot = s1 + s2
    out = jnp.where(lane == 0, e1, jnp.where(lane == 1, e2, jnp.where(lane == 2, s1 / tot, s2 / tot)))
    return jnp.where(lane < 4, out, 0.0)


def _merge_kernel(attn_ref, rec_ref, x_ref, on_ref, gate_ref, shift_ref, scale_ref, wout_ref,
                  rwh_ref, rwl_ref, rb_ref, xn_ref, h_ref, route_ref):
    attn = jnp.concatenate([attn_ref[h] for h in range(MLA_HEADS)], axis=1)
    ya = (_rms(attn) * on_ref[:, :MLA_WIDTH]).astype(BF16)
    yr = (_rms(rec_ref[...]) * on_ref[:, MLA_WIDTH:]).astype(BF16)
    y = _dot(ya, wout_ref[:MLA_WIDTH, :]) + _dot(yr, wout_ref[MLA_WIDTH:, :])
    xn = x_ref[...] + gate_ref[0] * y
    xn_ref[...] = xn
    h = _rms(xn) * (1.0 + scale_ref[0]) + shift_ref[0]
    h_ref[...] = h
    h_hi = h.astype(BF16)
    h_lo = (h - h_hi.astype(F32)).astype(BF16)
    logits = (_dot(h_hi, rwh_ref[...]) + _dot(h_lo, rwh_ref[...])) + _dot(h_hi, rwl_ref[...])
    route_ref[...] = _route(logits, rb_ref[...])


def _merge(attn, rec, x, mod, out_norm, w_out, rw_hi, rw_lo, rb, tpb):
    t = x.shape[0]
    tile = lambda n: pl.BlockSpec((TM, n), lambda i: (i, 0))
    return pl.pallas_call(
        _merge_kernel,
        grid=(t // TM,),
        in_specs=[
            pl.BlockSpec((MLA_HEADS, TM, V_HEAD), lambda i: (0, i, 0)), tile(LRU_WIDTH), tile(D_MODEL),
            pl.BlockSpec((1, D_MODEL), lambda i: (0, 0)),
            _mod_spec(2, tpb), _mod_spec(3, tpb), _mod_spec(4, tpb),
            _resident(w_out.shape), _resident(rw_hi.shape), _resident(rw_lo.shape),
            pl.BlockSpec((1, ROUTE_LANES), lambda i: (0, 0)),
        ],
        out_specs=[tile(D_MODEL), tile(D_MODEL), tile(ROUTE_LANES)],
        out_shape=[
            jax.ShapeDtypeStruct((t, D_MODEL), F32),
            jax.ShapeDtypeStruct((t, D_MODEL), F32),
            jax.ShapeDtypeStruct((t, ROUTE_LANES), F32),
        ],
        compiler_params=_params("arbitrary"),
        name="merge",
    )(attn, rec, x, out_norm, mod, mod, mod, w_out, rw_hi, rw_lo, rb)


def _row_copy(idx_ref, src_ref, dst_ref, sem, r):
    return pltpu.make_async_copy(src_ref.at[pl.ds(idx_ref[0, 0, r], 1)], dst_ref.at[pl.ds(r, 1)], sem)


def _start_rows(idx_ref, n, src_ref, dst_ref, sem):
    def body(r, c):
        _row_copy(idx_ref, src_ref, dst_ref, sem, r).start()
        return c

    lax.fori_loop(0, n, body, 0, unroll=DMA_UNROLL)


def _wait_rows(idx_ref, n, src_ref, dst_ref, sem):
    def body(r, c):
        _row_copy(idx_ref, src_ref, dst_ref, sem, r).wait()
        return c

    lax.fori_loop(0, n, body, 0, unroll=DMA_UNROLL)


def _expert_kernel(used_ref, te_ref, first_ref, nxt_ref, par_ref, tok_ref, tok_next_ref, h_hbm,
                   wg_hbm, wu_hbm, wd_hbm, o_ref, x_buf, wg_buf, wu_buf, wd_buf, xsem, wsem, *, layer):
    j = pl.program_id(0)
    used = used_ref[0]

    def fetch(expert, slot):
        return (pltpu.make_async_copy(wg_hbm.at[layer, expert], wg_buf.at[slot], wsem.at[slot, 0]),
                pltpu.make_async_copy(wu_hbm.at[layer, expert], wu_buf.at[slot], wsem.at[slot, 1]),
                pltpu.make_async_copy(wd_hbm.at[layer, expert], wd_buf.at[slot], wsem.at[slot, 2]))

    @pl.when(j < used)
    def _():
        slot = par_ref[j]
        xslot = lax.rem(j, 2)

        @pl.when(j == 0)
        def _():
            _start_rows(tok_ref, MOE_TILE, h_hbm, x_buf.at[0], xsem.at[0])
            for c in fetch(te_ref[0], 0):
                c.start(priority=WEIGHT_DMA_PRIORITY)

        @pl.when(j + 1 < used)
        def _():
            _start_rows(tok_next_ref, MOE_TILE, h_hbm, x_buf.at[1 - xslot], xsem.at[1 - xslot])

        @pl.when(first_ref[j] == 1)
        def _():
            for c in fetch(te_ref[j], slot):
                c.wait()

            @pl.when(nxt_ref[j] >= 0)
            def _():
                for c in fetch(nxt_ref[j], 1 - slot):
                    c.start(priority=WEIGHT_DMA_PRIORITY)

        _wait_rows(tok_ref, MOE_TILE, h_hbm, x_buf.at[xslot], xsem.at[xslot])
        x = x_buf[xslot].astype(BF16)
        acc = jnp.zeros((MOE_TILE, D_MODEL), F32)
        for c in range(D_FF_EXPERT // MOE_FF_CHUNK):
            cs = slice(c * MOE_FF_CHUNK, (c + 1) * MOE_FF_CHUNK)
            g = _dot(x, wg_buf[slot, :, cs].astype(BF16))
            u = _dot(x, wu_buf[slot, :, cs].astype(BF16))
            act = (jax.nn.silu(g) * u).astype(BF16)
            acc = acc + _dot(act, wd_buf[slot, cs, :].astype(BF16))
        o_ref[...] = acc

    @pl.when(j >= used)
    def _():
        o_ref[...] = jnp.zeros_like(o_ref)


def _experts(h, tok, plan, wg, wu, wd, layer):
    n_tiles = tok.shape[0]
    grid_spec = pltpu.PrefetchScalarGridSpec(
        num_scalar_prefetch=5,
        grid=(n_tiles,),
        in_specs=[
            pl.BlockSpec((1, 1, MOE_TILE), lambda j, *_: (j, 0, 0), memory_space=pltpu.SMEM),
            pl.BlockSpec((1, 1, MOE_TILE), lambda j, *_: (jnp.minimum(j + 1, n_tiles - 1), 0, 0),
                         memory_space=pltpu.SMEM),
            pl.BlockSpec(memory_space=pl.ANY),
            pl.BlockSpec(memory_space=pl.ANY),
            pl.BlockSpec(memory_space=pl.ANY),
            pl.BlockSpec(memory_space=pl.ANY),
        ],
        out_specs=pl.BlockSpec((MOE_TILE, D_MODEL), lambda j, *_: (j, 0)),
        scratch_shapes=[
            pltpu.VMEM((2, MOE_TILE, D_MODEL), F32),
            pltpu.VMEM((2, D_MODEL, D_FF_EXPERT), F32),
            pltpu.VMEM((2, D_MODEL, D_FF_EXPERT), F32),
            pltpu.VMEM((2, D_FF_EXPERT, D_MODEL), F32),
            pltpu.SemaphoreType.DMA((2,)),
            pltpu.SemaphoreType.DMA((2, 3)),
        ],
    )
    return pl.pallas_call(
        functools.partial(_expert_kernel, layer=layer),
        grid_spec=grid_spec,
        out_shape=jax.ShapeDtypeStruct((n_tiles * MOE_TILE, D_MODEL), F32),
        compiler_params=_params("arbitrary"),
        name="experts",
    )(plan['n_used'], plan['te'], plan['first'], plan['nxt'], plan['par'], tok, tok, h, wg, wu, wd)


def _combine_kernel(idx_ref, route_ref, xn_ref, gate_ref, y_hbm, o_ref, buf, sem):
    _start_rows(idx_ref, 2 * TM, y_hbm, buf, sem)
    _wait_rows(idx_ref, 2 * TM, y_hbm, buf, sem)
    y = buf[:TM, :] * route_ref[:, 2:3] + buf[TM:, :] * route_ref[:, 3:4]
    o_ref[...] = xn_ref[...] + gate_ref[0] * y


def _combine(slots, route, xn, mod, y_sorted, tpb):
    t = xn.shape[0]
    return pl.pallas_call(
        _combine_kernel,
        grid=(t // TM,),
        in_specs=[
            pl.BlockSpec((1, 1, 2 * TM), lambda i: (i, 0, 0), memory_space=pltpu.SMEM),
            pl.BlockSpec((TM, ROUTE_LANES), lambda i: (i, 0)),
            pl.BlockSpec((TM, D_MODEL), lambda i: (i, 0)),
            _mod_spec(5, tpb),
            pl.BlockSpec(memory_space=pl.ANY),
        ],
        out_specs=pl.BlockSpec((TM, D_MODEL), lambda i: (i, 0)),
        out_shape=jax.ShapeDtypeStruct((t, D_MODEL), F32),
        scratch_shapes=[pltpu.VMEM((2 * TM, D_MODEL), F32), pltpu.SemaphoreType.DMA(())],
        compiler_params=_params("arbitrary"),
        name="combine",
    )(slots, route, xn, mod, y_sorted)


def _dispatch(route):
    t = route.shape[0]
    flat_e = route[:, :2].astype(jnp.int32).reshape(-1)
    n_pairs = flat_e.shape[0]
    experts = jnp.arange(N_EXPERTS, dtype=jnp.int32)
    onehot = (flat_e[:, None] == experts[None, :]).astype(jnp.int32)
    csum = jnp.cumsum(onehot, axis=0)
    rank = jnp.sum(csum * onehot, axis=1) - 1
    counts = csum[-1]
    padded = (counts + MOE_TILE - 1) // MOE_TILE * MOE_TILE
    pend = jnp.cumsum(padded)
    slot = jnp.sum((pend - padded)[None, :] * onehot, axis=1) + rank
    n_tiles = -(-n_pairs // MOE_TILE) + N_EXPERTS
    n_used = (pend[-1] // MOE_TILE).astype(jnp.int32)
    tile_ids = jnp.arange(n_tiles, dtype=jnp.int32)
    used = tile_ids < n_used
    te = jnp.minimum(jnp.sum((pend[None, :] <= (tile_ids * MOE_TILE)[:, None]).astype(jnp.int32), axis=1),
                     N_EXPERTS - 1)
    prev = jnp.concatenate([jnp.full((1,), -1, jnp.int32), te[:-1]])
    first = jnp.logical_and(used, te != prev).astype(jnp.int32)
    par = (jnp.cumsum(first) - 1) % 2
    later = jnp.logical_and(experts[None, :] > experts[:, None], (counts > 0)[None, :])
    next_expert = jnp.min(jnp.where(later, experts[None, :], N_EXPERTS), axis=1)
    next_expert = jnp.where(next_expert == N_EXPERTS, -1, next_expert)
    tok = jnp.zeros((n_tiles * MOE_TILE,), jnp.int32).at[slot].set(jnp.arange(n_pairs, dtype=jnp.int32) // 2)
    slots = slot.reshape(t // TM, TM, 2).transpose(0, 2, 1).reshape(t // TM, 1, 2 * TM)
    plan = dict(n_used=n_used.reshape(1), te=te.astype(jnp.int32), first=first,
                nxt=next_expert[te].astype(jnp.int32), par=par.astype(jnp.int32))
    return slots, tok.reshape(n_tiles, 1, MOE_TILE), plan


def _rope_tables(n_latent):
    rows = n_latent // GRID_W
    row = jnp.repeat(jnp.arange(rows), GRID_W).astype(F32)
    col = jnp.tile(jnp.arange(GRID_W), rows).astype(F32)
    n_freq = QK_ROPE // 4
    inv_freq = ROPE_BASE ** (-jnp.arange(n_freq, dtype=F32) / n_freq)
    ar, ac = row[:, None] * inv_freq[None, :], col[:, None] * inv_freq[None, :]
    zeros = jnp.zeros((n_latent, LANES - QK_ROPE), F32)
    cos = jnp.concatenate([jnp.cos(ar), jnp.cos(ar), jnp.cos(ac), jnp.cos(ac), zeros], axis=1)
    sin = jnp.concatenate([-jnp.sin(ar), jnp.sin(ar), -jnp.sin(ac), jnp.sin(ac), zeros], axis=1)
    ctx_cos = jnp.concatenate([jnp.ones((TM, QK_ROPE), F32), jnp.zeros((TM, LANES - QK_ROPE), F32)], axis=1)
    return (jnp.concatenate([ctx_cos, cos], axis=0),
            jnp.concatenate([jnp.zeros((TM, LANES), F32), sin], axis=0))


def _pad_cols(w, n):
    return jnp.concatenate([w, jnp.zeros(w.shape[:-1] + (n,), w.dtype)], axis=-1)


def _layer_params(l, w_in, q_a_norm, w_uq, kv_a_norm, w_ukv, q_norm, k_norm, conv_w, conv_b, lru_wa, lru_ba,
                  lru_wx, lru_bx, lru_lambda, out_norm, w_out):
    s0, s1, s2, s3 = Q_RANK, Q_RANK + KV_RANK, Q_RANK + KV_RANK + QK_ROPE, Q_RANK + KV_RANK + QK_ROPE + LRU_WIDTH
    w = w_in[l]
    w_in_p = jnp.concatenate([w[:, s2:s3], w[:, s3:], w[:, :s0], w[:, s0:s1],
                              _pad_cols(w[:, s1:s2], LANES - QK_ROPE)], axis=1).astype(BF16)
    wq = w_uq[l].reshape(Q_RANK, MLA_HEADS, QK_HEAD)
    wuq_p = _pad_cols(wq, QK_PAD - QK_HEAD).reshape(Q_RANK, MLA_HEADS * QK_PAD).astype(BF16)
    wkv = w_ukv[l].reshape(KV_RANK, MLA_HEADS, QK_NOPE + V_HEAD)
    wukv_p = jnp.concatenate([wkv[:, :, :QK_NOPE].reshape(KV_RANK, -1),
                              wkv[:, :, QK_NOPE:].reshape(KV_RANK, -1)], axis=1).astype(BF16)
    w_gate = [jnp.concatenate([lru_wa[l, d], lru_wx[l, d]], axis=-1).astype(BF16) for d in range(2)]
    return dict(
        w_in=w_in_p, wuq=wuq_p, wukv=wukv_p,
        qa=q_a_norm[l][None], kva=kv_a_norm[l][None],
        qn=_pad_cols(q_norm[l], QK_PAD - QK_HEAD)[None], kn=_pad_cols(k_norm[l], QK_PAD - QK_HEAD)[None],
        conv_w=conv_w[l], conv_b=conv_b[l][None], w_gate=w_gate,
        ba=[lru_ba[l, d][None] for d in range(2)], bx=[lru_bx[l, d][None] for d in range(2)],
        lam=[lru_lambda[l, d][None] for d in range(2)],
        out_norm=out_norm[l][None], w_out=w_out[l].astype(BF16),
    )


def kernel(x, c, ctx, c_ctx, ada_w, ada_b, w_in, q_a_norm, w_uq, kv_a_norm, w_ukv, q_norm, k_norm, conv_w, conv_b, lru_wa, lru_ba, lru_wx, lru_bx, lru_lambda, out_norm, w_out, router_w, router_b, moe_w_gate, moe_w_up, moe_w_down):
    batch, n, d = x.shape
    depth = ada_w.shape[0]
    assert d == D_MODEL and ctx.shape[1] == TM and n % ATTN_KV_CHUNK == 0 and n % GRID_W == 0
    assert 1 + batch <= SUBLANES
    tpb = (TM + n) // TM
    t = batch * tpb * TM

    cvec = jnp.concatenate([c_ctx[None], c, jnp.zeros((SUBLANES - 1 - batch, d), F32)], axis=0)
    mod_all = _adaln(cvec, ada_w, ada_b)
    cos_t, sin_t = _rope_tables(n)
    rw = _pad_cols(router_w, ROUTE_LANES - N_EXPERTS)
    rw_hi = rw.astype(BF16)
    rw_lo = (rw - rw_hi.astype(F32)).astype(BF16)
    rb = _pad_cols(router_b, ROUTE_LANES - N_EXPERTS)[None]

    xs = jnp.concatenate([ctx, x], axis=1).reshape(t, d)
    for l in range(depth):
        p = _layer_params(l, w_in, q_a_norm, w_uq, kv_a_norm, w_ukv, q_norm, k_norm, conv_w, conv_b, lru_wa,
                          lru_ba, lru_wx, lru_bx, lru_lambda, out_norm, w_out)
        mod = mod_all[l].reshape(SUBLANES * 6, 1, d)
        lru_in, mla_in = _inproj(xs, mod, p['w_in'], tpb)
        q, k, vt = _mla_prep(mla_in, p['qa'], p['kva'], p['qn'], p['kn'], cos_t, sin_t, p['wuq'], p['wukv'], tpb)
        attn = _attention(q, k, vt, batch, tpb)
        h_fwd = _lru(lru_in, p['conv_w'], p['conv_b'], p['w_gate'][0], p['ba'][0], p['bx'][0], p['lam'][0],
                     batch, tpb, False)
        rec = _lru(lru_in, p['conv_w'], p['conv_b'], p['w_gate'][1], p['ba'][1], p['bx'][1], p['lam'][1],
                   batch, tpb, True, h_fwd)
        xn, h2, route = _merge(attn, rec, xs, mod, p['out_norm'], p['w_out'], rw_hi, rw_lo, rb, tpb)
        slots, tok, plan = _dispatch(route)
        y_sorted = _experts(h2, tok, plan, moe_w_gate, moe_w_up, moe_w_down, l)
        xs = _combine(slots, route, xn, mod, y_sorted, tpb)
    return xs.reshape(batch, tpb * TM, d)[:, TM:]
```

```python
import functools

import jax
import jax.numpy as jnp
from jax import lax
from jax.experimental import pallas as pl
from jax.experimental.pallas import tpu as pltpu

F32 = jnp.float32
BF16 = jnp.bfloat16

D_MODEL = 2048
MLA_HEADS = 8
QK_NOPE = 128
QK_ROPE = 64
QK_HEAD = QK_NOPE + QK_ROPE
V_HEAD = 128
Q_RANK = 512
KV_RANK = 256
MLA_WIDTH = MLA_HEADS * V_HEAD
LRU_WIDTH = D_MODEL - MLA_WIDTH
LRU_HEADS = 8
LRU_HEAD_DIM = LRU_WIDTH // LRU_HEADS
CONV_W = 4
LRU_C = 8.0
N_EXPERTS = 32
N_GROUPS = 4
EXPERTS_PER_GROUP = N_EXPERTS // N_GROUPS
D_FF_EXPERT = 768
GRID_W = 64
ROPE_BASE = 10000.0
EPS = 1e-6
LOG2_E = 1.4426950408889634

LANES = 128
SUBLANES = 8
VMEM_LIMIT = 56 * 1024 * 1024

TM = 256
QK_PAD = 2 * LANES
MLA_COLS = Q_RANK + KV_RANK + LANES
ATTN_KV_CHUNK = 512
MOE_TILE = 256
MOE_FF_CHUNK = 256
ROUTE_LANES = LANES
DMA_UNROLL = 8
WEIGHT_DMA_PRIORITY = 1

_NT = (((1,), (1,)), ((), ()))


def _dot(a, b):
    return jnp.dot(a, b, preferred_element_type=F32)


def _rms(x, denom=None):
    d = x.shape[-1] if denom is None else denom
    ms = jnp.sum(x * x, axis=-1, keepdims=True) * (1.0 / d)
    return x * lax.rsqrt(ms + EPS)


def _params(*sem):
    return pltpu.CompilerParams(dimension_semantics=sem, vmem_limit_bytes=VMEM_LIMIT)


def _resident(shape):
    nd = len(shape)
    return pl.BlockSpec(shape, lambda *_: (0,) * nd, pipeline_mode=pl.Buffered(1))


def _adaln_kernel(c_ref, w_ref, b_ref, o_ref):
    s = jax.nn.silu(c_ref[...]).astype(BF16)
    o_ref[0] = _dot(s, w_ref[0].astype(BF16)) + b_ref[0]


def _adaln(cvec, ada_w, ada_b):
    depth, d, n6 = ada_w.shape
    rows = cvec.shape[0]
    tn = 1024
    return pl.pallas_call(
        _adaln_kernel,
        grid=(depth, n6 // tn),
        in_specs=[
            pl.BlockSpec((rows, d), lambda l, j: (0, 0)),
            pl.BlockSpec((1, d, tn), lambda l, j: (l, 0, j)),
            pl.BlockSpec((1, 1, tn), lambda l, j: (l, 0, j)),
        ],
        out_specs=pl.BlockSpec((1, rows, tn), lambda l, j: (l, 0, j)),
        out_shape=jax.ShapeDtypeStruct((depth, rows, n6), F32),
        compiler_params=_params("arbitrary", "arbitrary"),
        name="adaln",
    )(cvec, ada_w, ada_b.reshape(depth, 1, n6))


def _seg(i, tiles_per_batch):
    return jnp.where(i % tiles_per_batch == 0, 0, 1 + i // tiles_per_batch)


def _mod_spec(chunk, tiles_per_batch):
    return pl.BlockSpec((1, 1, D_MODEL), lambda i: (_seg(i, tiles_per_batch) * 6 + chunk, 0, 0))


def _inproj_kernel(x_ref, shift_ref, scale_ref, w_ref, lru_ref, mla_ref):
    h = _rms(x_ref[...]) * (1.0 + scale_ref[0]) + shift_ref[0]
    y = _dot(h.astype(BF16), w_ref[...])
    lru_ref[...] = y[:, :2 * LRU_WIDTH]
    mla_ref[...] = y[:, 2 * LRU_WIDTH:]


def _inproj(x, mod, w_in_p, tpb):
    t = x.shape[0]
    n_out = w_in_p.shape[1]
    return pl.pallas_call(
        _inproj_kernel,
        grid=(t // TM,),
        in_specs=[
            pl.BlockSpec((TM, D_MODEL), lambda i: (i, 0)),
            _mod_spec(0, tpb),
            _mod_spec(1, tpb),
            _resident((D_MODEL, n_out)),
        ],
        out_specs=[
            pl.BlockSpec((TM, 2 * LRU_WIDTH), lambda i: (i, 0)),
            pl.BlockSpec((TM, MLA_COLS), lambda i: (i, 0)),
        ],
        out_shape=[
            jax.ShapeDtypeStruct((t, 2 * LRU_WIDTH), F32),
            jax.ShapeDtypeStruct((t, MLA_COLS), F32),
        ],
        compiler_params=_params("arbitrary"),
        name="inproj",
    )(x, mod, mod, w_in_p)


def _swap_halves(x):
    lane = lax.broadcasted_iota(jnp.int32, x.shape, 1)
    up = pltpu.roll(x, LANES - 16, 1)
    down = pltpu.roll(x, 16, 1)
    return jnp.where((lane & 31) < 16, up, down)


def _mla_prep_kernel(p_ref, qa_ref, kva_ref, qn_ref, kn_ref, cos_ref, sin_ref, wuq_ref, wukv_ref,
                     q_ref, k_ref, vt_ref):
    cos = cos_ref[...]
    sin = sin_ref[...]
    scale = QK_HEAD ** -0.5 * LOG2_E

    def rope(x):
        return x * cos + _swap_halves(x) * sin

    cq = p_ref[:, :Q_RANK]
    qf = _dot((_rms(cq) * qa_ref[...]).astype(BF16), wuq_ref[...])
    ckv = p_ref[:, Q_RANK:Q_RANK + KV_RANK]
    kvf = _dot((_rms(ckv) * kva_ref[...]).astype(BF16), wukv_ref[...])
    kr = p_ref[:, Q_RANK + KV_RANK:]
    k_rope = rope(_rms(kr, QK_ROPE) * kn_ref[:, QK_NOPE:]).astype(BF16)
    for h in range(MLA_HEADS):
        c0 = h * QK_PAD
        q_nope = _rms(qf[:, c0:c0 + QK_NOPE]) * qn_ref[:, :QK_NOPE]
        q_rope = rope(_rms(qf[:, c0 + QK_NOPE:c0 + QK_PAD], QK_ROPE) * qn_ref[:, QK_NOPE:])
        q_ref[h, :, :QK_NOPE] = (q_nope * scale).astype(BF16)
        q_ref[h, :, QK_NOPE:] = (q_rope * scale).astype(BF16)
        k_nope = _rms(kvf[:, h * QK_NOPE:(h + 1) * QK_NOPE]) * kn_ref[:, :QK_NOPE]
        k_ref[h, :, :QK_NOPE] = k_nope.astype(BF16)
        k_ref[h, :, QK_NOPE:] = k_rope
        v0 = MLA_HEADS * QK_NOPE + h * V_HEAD
        vt_ref[h] = kvf[:, v0:v0 + V_HEAD].T.astype(BF16)


def _mla_prep(mla_in, qa, kva, qn, kn, cos_t, sin_t, wuq_p, wukv_p, tpb):
    t = mla_in.shape[0]
    row = lambda n: pl.BlockSpec((1, n), lambda i: (0, 0))
    return pl.pallas_call(
        _mla_prep_kernel,
        grid=(t // TM,),
        in_specs=[
            pl.BlockSpec((TM, MLA_COLS), lambda i: (i, 0)),
            row(Q_RANK), row(KV_RANK), row(QK_PAD), row(QK_PAD),
            pl.BlockSpec((TM, LANES), lambda i: (i % tpb, 0)),
            pl.BlockSpec((TM, LANES), lambda i: (i % tpb, 0)),
            _resident(wuq_p.shape),
            _resident(wukv_p.shape),
        ],
        out_specs=[
            pl.BlockSpec((MLA_HEADS, TM, QK_PAD), lambda i: (0, i, 0)),
            pl.BlockSpec((MLA_HEADS, TM, QK_PAD), lambda i: (0, i, 0)),
            pl.BlockSpec((MLA_HEADS, V_HEAD, TM), lambda i: (0, 0, i)),
        ],
        out_shape=[
            jax.ShapeDtypeStruct((MLA_HEADS, t, QK_PAD), BF16),
            jax.ShapeDtypeStruct((MLA_HEADS, t, QK_PAD), BF16),
            jax.ShapeDtypeStruct((MLA_HEADS, V_HEAD, t), BF16),
        ],
        compiler_params=_params("arbitrary"),
        name="mla_prep",
    )(mla_in, qa, kva, qn, kn, cos_t, sin_t, wuq_p, wukv_p)


def _column_reduce(x, op):
    return op(x.reshape(x.shape[0] // SUBLANES, SUBLANES, x.shape[1]), axis=0)


def _attn_kernel(q_ref, k_ref, vt_ref, o_ref, s0_ref, s1_ref, *, latent_chunks):
    is_ctx = pl.program_id(1) == 0
    ctx_only = ((0, TM),)
    every_key = ctx_only + latent_chunks

    def score_pass(j, s_ref, chunks):
        q = q_ref[j]
        m8 = None
        for start, size in chunks:
            s = lax.dot_general(k_ref[j, start:start + size, :], q, _NT, preferred_element_type=F32)
            s_ref[start:start + size, :] = s
            cm = _column_reduce(s, jnp.max)
            m8 = cm if m8 is None else jnp.maximum(m8, cm)
        return jnp.max(m8, axis=0, keepdims=True)

    def value_pass(j, m, s_ref, chunks):
        l8 = jnp.zeros((SUBLANES, TM), F32)
        acc = jnp.zeros((V_HEAD, TM), F32)
        for start, size in chunks:
            e = jnp.exp2(s_ref[start:start + size, :] - m)
            l8 = l8 + _column_reduce(e, jnp.sum)
            acc = acc + _dot(vt_ref[j, :, start:start + size], e.astype(BF16))
        l = jnp.sum(l8, axis=0, keepdims=True)
        o_ref[j] = (acc / l).T

    def all_heads(chunks):
        def pair(i, m_even):
            h = 2 * i
            m_odd = score_pass(h + 1, s1_ref, chunks)
            value_pass(h, m_even, s0_ref, chunks)
            m_even = score_pass(h + 2, s0_ref, chunks)
            value_pass(h + 1, m_odd, s1_ref, chunks)
            return m_even

        m_even = lax.fori_loop(0, MLA_HEADS // 2 - 1, pair, score_pass(0, s0_ref, chunks))
        m_odd = score_pass(MLA_HEADS - 1, s1_ref, chunks)
        value_pass(MLA_HEADS - 2, m_even, s0_ref, chunks)
        value_pass(MLA_HEADS - 1, m_odd, s1_ref, chunks)

    @pl.when(is_ctx)
    def _():
        all_heads(ctx_only)

    @pl.when(jnp.logical_not(is_ctx))
    def _():
        all_heads(every_key)


def _attention(q, k, vt, batch, tpb):
    t = q.shape[1]
    rows = tpb * TM
    n_chunks = (rows - TM) // ATTN_KV_CHUNK
    latent_chunks = tuple((TM + c * ATTN_KV_CHUNK, ATTN_KV_CHUNK) for c in range(n_chunks))
    once = pl.Buffered(1)
    return pl.pallas_call(
        functools.partial(_attn_kernel, latent_chunks=latent_chunks),
        grid=(batch, tpb),
        in_specs=[
            pl.BlockSpec((MLA_HEADS, TM, QK_PAD), lambda b, i: (0, b * tpb + i, 0)),
            pl.BlockSpec((MLA_HEADS, rows, QK_PAD), lambda b, i: (0, b, 0), pipeline_mode=once),
            pl.BlockSpec((MLA_HEADS, V_HEAD, rows), lambda b, i: (0, 0, b), pipeline_mode=once),
        ],
        out_specs=pl.BlockSpec((MLA_HEADS, TM, V_HEAD), lambda b, i: (0, b * tpb + i, 0)),
        out_shape=jax.ShapeDtypeStruct((MLA_HEADS, t, V_HEAD), F32),
        scratch_shapes=[pltpu.VMEM((rows, TM), F32), pltpu.VMEM((rows, TM), F32)],
        compiler_params=_params("arbitrary", "arbitrary"),
        name="attention",
    )(q, k, vt)


def _shift_rows(x, s, fill, reverse):
    rows = x.shape[0]
    pad = jnp.full((s, x.shape[1]), fill, x.dtype)
    if s % SUBLANES == 0:
        return jnp.concatenate([x[s:], pad], 0) if reverse else jnp.concatenate([pad, x[:rows - s]], 0)
    row = lax.broadcasted_iota(jnp.int32, x.shape, 0)
    if reverse:
        return jnp.where(row < rows - s, pltpu.roll(x, rows - s, 0), fill)
    return jnp.where(row >= s, pltpu.roll(x, s, 0), fill)


def _lru_kernel(*refs, reverse, tpb):
    if reverse:
        (x_ref, prev_ref, next_ref, cw_ref, cb_ref, w_ref, ba_ref, bx_ref, lam_ref, hf_ref, xg_ref,
         o_ref, carry_ref) = refs
    else:
        (x_ref, prev_ref, next_ref, cw_ref, cb_ref, w_ref, ba_ref, bx_ref, lam_ref,
         o_ref, carry_ref) = refs
    step = pl.program_id(1)
    tile = _lru_tile(step, tpb, reverse)

    @pl.when(step == 0)
    def _():
        carry_ref[...] = jnp.zeros_like(carry_ref)

    x = x_ref[...]
    row = lax.broadcasted_iota(jnp.int32, x.shape, 0)
    at_start = tile <= 1
    at_end = jnp.logical_or(tile == 0, tile == tpb - 1)
    before1 = jnp.where(at_start, 0.0, prev_ref[SUBLANES - 1:SUBLANES, :])
    before2 = jnp.where(at_start, 0.0, prev_ref[SUBLANES - 2:SUBLANES - 1, :])
    after1 = jnp.where(at_end, 0.0, next_ref[0:1, :])
    x_m1 = jnp.where(row == 0, before1, pltpu.roll(x, 1, 0))
    x_m2 = jnp.where(row == 0, before2, jnp.where(row == 1, before1, pltpu.roll(x, 2, 0)))
    x_p1 = jnp.where(row == TM - 1, after1, pltpu.roll(x, TM - 1, 0))
    u = cb_ref[...] + (((x_m2 * cw_ref[0:1, :] + x_m1 * cw_ref[1:2, :]) + x * cw_ref[2:3, :])
                       + x_p1 * cw_ref[3:4, :])

    decay = -LRU_C * jax.nn.softplus(-lam_ref[...])
    for h in range(LRU_HEADS):
        c0 = h * LRU_HEAD_DIM
        sl = slice(c0, c0 + LRU_HEAD_DIM)
        uh = u[:, sl]
        z = _dot(uh.astype(BF16), w_ref[h])
        r = jax.nn.sigmoid(z[:, :LRU_HEAD_DIM] + ba_ref[:, sl])
        i = jax.nn.sigmoid(z[:, LRU_HEAD_DIM:] + bx_ref[:, sl])
        log_a = r * decay[:, sl]
        a = jnp.exp(log_a)
        b = jnp.sqrt(-jnp.tanh(log_a) * (1.0 + a * a)) * (i * uh)
        s = 1
        while s < TM:
            b = a * _shift_rows(b, s, 0.0, reverse) + b
            a = a * _shift_rows(a, s, 1.0, reverse)
            s *= 2
        hcur = b + a * carry_ref[0:1, sl]
        edge = 0 if reverse else TM - 1
        carry_ref[0:1, sl] = hcur[edge:edge + 1, :]
        if reverse:
            o_ref[:, sl] = jax.nn.gelu(xg_ref[:, sl]) * (hf_ref[:, sl] + hcur)
        else:
            o_ref[:, sl] = hcur


def _lru_tile(step, tpb, reverse):
    if not reverse:
        return step
    return jnp.where(step == 0, 0, tpb - step)


def _lru(lru_in, conv_w, conv_b, w_gate, ba, bx, lam, batch, tpb, reverse, h_fwd=None):
    t = lru_in.shape[0]
    halo_blocks = t // SUBLANES
    per_tile = TM // SUBLANES

    def tile_idx(b, s):
        return b * tpb + _lru_tile(s, tpb, reverse)

    row = lambda n: pl.BlockSpec((1, n), lambda b, s: (0, 0))
    in_specs = [
        pl.BlockSpec((TM, LRU_WIDTH), lambda b, s: (tile_idx(b, s), 0)),
        pl.BlockSpec((SUBLANES, LRU_WIDTH), lambda b, s: (jnp.maximum(tile_idx(b, s) * per_tile - 1, 0), 0)),
        pl.BlockSpec((SUBLANES, LRU_WIDTH),
                     lambda b, s: (jnp.minimum((tile_idx(b, s) + 1) * per_tile, halo_blocks - 1), 0)),
        pl.BlockSpec((CONV_W, LRU_WIDTH), lambda b, s: (0, 0)),
        row(LRU_WIDTH),
        pl.BlockSpec((LRU_HEADS, LRU_HEAD_DIM, 2 * LRU_HEAD_DIM), lambda b, s: (0, 0, 0)),
        row(LRU_WIDTH), row(LRU_WIDTH), row(LRU_WIDTH),
    ]
    args = [lru_in, lru_in, lru_in, conv_w, conv_b, w_gate, ba, bx, lam]
    if reverse:
        in_specs += [
            pl.BlockSpec((TM, LRU_WIDTH), lambda b, s: (tile_idx(b, s), 0)),
            pl.BlockSpec((TM, LRU_WIDTH), lambda b, s: (tile_idx(b, s), 1)),
        ]
        args += [h_fwd, lru_in]
    return pl.pallas_call(
        functools.partial(_lru_kernel, reverse=reverse, tpb=tpb),
        grid=(batch, tpb),
        in_specs=in_specs,
        out_specs=pl.BlockSpec((TM, LRU_WIDTH), lambda b, s: (tile_idx(b, s), 0)),
        out_shape=jax.ShapeDtypeStruct((t, LRU_WIDTH), F32),
        scratch_shapes=[pltpu.VMEM((SUBLANES, LRU_WIDTH), F32)],
        compiler_params=_params("arbitrary", "arbitrary"),
        name="lru_rev" if reverse else "lru_fwd",
    )(*args)


def _first_index(mask, lane_f):
    return jnp.min(jnp.where(mask, lane_f, float(ROUTE_LANES)), axis=-1, keepdims=True)


def _route(logits, rb):
    lane = lax.broadcasted_iota(jnp.int32, logits.shape, 1)
    lane_f = lane.astype(F32)
    group = lane >> 3
    scores = jax.nn.sigmoid(logits)
    biased = scores + rb
    neg = -jnp.inf
    best = e1 = e2 = None
    for g in range(N_GROUPS):
        vg = jnp.where(group == g, biased, neg)
        m1 = jnp.max(vg, axis=-1, keepdims=True)
        i1 = _first_index(vg == m1, lane_f)
        vg2 = jnp.where(lane_f == i1, neg, vg)
        m2 = jnp.max(vg2, axis=-1, keepdims=True)
        i2 = _first_index(vg2 == m2, lane_f)
        gs = m1 + m2
        if g == 0:
            best, e1, e2 = gs, i1, i2
        else:
            better = gs > best
            best = jnp.where(better, gs, best)
            e1 = jnp.where(better, i1, e1)
            e2 = jnp.where(better, i2, e2)
    s1 = jnp.sum(jnp.where(lane_f == e1, scores, 0.0), axis=-1, keepdims=True)
    s2 = jnp.sum(jnp.where(lane_f == e2, scores, 0.0), axis=-1, keepdims=True)
    tot = s1 + s2
    out = jnp.where(lane == 0, e1, jnp.where(lane == 1, e2, jnp.where(lane == 2, s1 / tot, s2 / tot)))
    return jnp.where(lane < 4, out, 0.0)


def _merge_kernel(attn_ref, rec_ref, x_ref, on_ref, gate_ref, shift_ref, scale_ref, wout_ref,
                  rwh_ref, rwl_ref, rb_ref, xn_ref, h_ref, route_ref):
    attn = jnp.concatenate([attn_ref[h] for h in range(MLA_HEADS)], axis=1)
    ya = (_rms(attn) * on_ref[:, :MLA_WIDTH]).astype(BF16)
    yr = (_rms(rec_ref[...]) * on_ref[:, MLA_WIDTH:]).astype(BF16)
    y = _dot(ya, wout_ref[:MLA_WIDTH, :]) + _dot(yr, wout_ref[MLA_WIDTH:, :])
    xn = x_ref[...] + gate_ref[0] * y
    xn_ref[...] = xn
    h = _rms(xn) * (1.0 + scale_ref[0]) + shift_ref[0]
    h_ref[...] = h
    h_hi = h.astype(BF16)
    h_lo = (h - h_hi.astype(F32)).astype(BF16)
    logits = (_dot(h_hi, rwh_ref[...]) + _dot(h_lo, rwh_ref[...])) + _dot(h_hi, rwl_ref[...])
    route_ref[...] = _route(logits, rb_ref[...])


def _merge(attn, rec, x, mod, out_norm, w_out, rw_hi, rw_lo, rb, tpb):
    t = x.shape[0]
    tile = lambda n: pl.BlockSpec((TM, n), lambda i: (i, 0))
    return pl.pallas_call(
        _merge_kernel,
        grid=(t // TM,),
        in_specs=[
            pl.BlockSpec((MLA_HEADS, TM, V_HEAD), lambda i: (0, i, 0)), tile(LRU_WIDTH), tile(D_MODEL),
            pl.BlockSpec((1, D_MODEL), lambda i: (0, 0)),
            _mod_spec(2, tpb), _mod_spec(3, tpb), _mod_spec(4, tpb),
            _resident(w_out.shape), _resident(rw_hi.shape), _resident(rw_lo.shape),
            pl.BlockSpec((1, ROUTE_LANES), lambda i: (0, 0)),
        ],
        out_specs=[tile(D_MODEL), tile(D_MODEL), tile(ROUTE_LANES)],
        out_shape=[
            jax.ShapeDtypeStruct((t, D_MODEL), F32),
            jax.ShapeDtypeStruct((t, D_MODEL), F32),
            jax.ShapeDtypeStruct((t, ROUTE_LANES), F32),
        ],
        compiler_params=_params("arbitrary"),
        name="merge",
    )(attn, rec, x, out_norm, mod, mod, mod, w_out, rw_hi, rw_lo, rb)


def _row_copy(idx_ref, src_ref, dst_ref, sem, r):
    return pltpu.make_async_copy(src_ref.at[pl.ds(idx_ref[0, 0, r], 1)], dst_ref.at[pl.ds(r, 1)], sem)


def _start_rows(idx_ref, n, src_ref, dst_ref, sem):
    def body(g, c):
        for u in range(DMA_UNROLL):
            _row_copy(idx_ref, src_ref, dst_ref, sem, g * DMA_UNROLL + u).start()
        return c

    lax.fori_loop(0, (n + DMA_UNROLL - 1) // DMA_UNROLL, body, 0)


def _wait_rows(idx_ref, n, src_ref, dst_ref, sem):
    def body(g, c):
        for u in range(DMA_UNROLL):
            _row_copy(idx_ref, src_ref, dst_ref, sem, g * DMA_UNROLL + u).wait()
        return c

    lax.fori_loop(0, (n + DMA_UNROLL - 1) // DMA_UNROLL, body, 0)


def _expert_kernel(used_ref, te_ref, first_ref, nxt_ref, par_ref, valid_ref, tok_ref, tok_next_ref, h_hbm,
                   wg_hbm, wu_hbm, wd_hbm, o_ref, x_buf, wg_buf, wu_buf, wd_buf, xsem, wsem, *, layer):
    j = pl.program_id(0)
    used = used_ref[0]

    def fetch(expert, slot):
        return (pltpu.make_async_copy(wg_hbm.at[layer, expert], wg_buf.at[slot], wsem.at[slot, 0]),
                pltpu.make_async_copy(wu_hbm.at[layer, expert], wu_buf.at[slot], wsem.at[slot, 1]),
                pltpu.make_async_copy(wd_hbm.at[layer, expert], wd_buf.at[slot], wsem.at[slot, 2]))

    @pl.when(j < used)
    def _():
        slot = par_ref[j]
        xslot = lax.rem(j, 2)

        @pl.when(j == 0)
        def _():
            x_buf[...] = jnp.zeros_like(x_buf)
            _start_rows(tok_ref, valid_ref[0], h_hbm, x_buf.at[0], xsem.at[0])
            for c in fetch(te_ref[0], 0):
                c.start(priority=WEIGHT_DMA_PRIORITY)

        @pl.when(j + 1 < used)
        def _():
            _start_rows(tok_next_ref, valid_ref[j + 1], h_hbm, x_buf.at[1 - xslot], xsem.at[1 - xslot])

        @pl.when(first_ref[j] == 1)
        def _():
            for c in fetch(te_ref[j], slot):
                c.wait()

            @pl.when(nxt_ref[j] >= 0)
            def _():
                for c in fetch(nxt_ref[j], 1 - slot):
                    c.start(priority=WEIGHT_DMA_PRIORITY)

        _wait_rows(tok_ref, valid_ref[j], h_hbm, x_buf.at[xslot], xsem.at[xslot])
        x = x_buf[xslot].astype(BF16)
        acc = jnp.zeros((MOE_TILE, D_MODEL), F32)
        for c in range(D_FF_EXPERT // MOE_FF_CHUNK):
            cs = slice(c * MOE_FF_CHUNK, (c + 1) * MOE_FF_CHUNK)
            g = _dot(x, wg_buf[slot, :, cs].astype(BF16))
            u = _dot(x, wu_buf[slot, :, cs].astype(BF16))
            act = (jax.nn.silu(g) * u).astype(BF16)
            acc = acc + _dot(act, wd_buf[slot, cs, :].astype(BF16))
        o_ref[...] = acc

    @pl.when(j >= used)
    def _():
        o_ref[...] = jnp.zeros_like(o_ref)


def _experts(h, tok, plan, wg, wu, wd, layer):
    n_tiles = tok.shape[0]
    grid_spec = pltpu.PrefetchScalarGridSpec(
        num_scalar_prefetch=6,
        grid=(n_tiles,),
        in_specs=[
            pl.BlockSpec((1, 1, MOE_TILE), lambda j, *_: (j, 0, 0), memory_space=pltpu.SMEM),
            pl.BlockSpec((1, 1, MOE_TILE), lambda j, *_: (jnp.minimum(j + 1, n_tiles - 1), 0, 0),
                         memory_space=pltpu.SMEM),
            pl.BlockSpec(memory_space=pl.ANY),
            pl.BlockSpec(memory_space=pl.ANY),
            pl.BlockSpec(memory_space=pl.ANY),
            pl.BlockSpec(memory_space=pl.ANY),
        ],
        out_specs=pl.BlockSpec((MOE_TILE, D_MODEL), lambda j, *_: (j, 0)),
        scratch_shapes=[
            pltpu.VMEM((2, MOE_TILE, D_MODEL), F32),
            pltpu.VMEM((2, D_MODEL, D_FF_EXPERT), F32),
            pltpu.VMEM((2, D_MODEL, D_FF_EXPERT), F32),
            pltpu.VMEM((2, D_FF_EXPERT, D_MODEL), F32),
            pltpu.SemaphoreType.DMA((2,)),
            pltpu.SemaphoreType.DMA((2, 3)),
        ],
    )
    return pl.pallas_call(
        functools.partial(_expert_kernel, layer=layer),
        grid_spec=grid_spec,
        out_shape=jax.ShapeDtypeStruct((n_tiles * MOE_TILE, D_MODEL), F32),
        compiler_params=_params("arbitrary"),
        name="experts",
    )(plan['n_used'], plan['te'], plan['first'], plan['nxt'], plan['par'], plan['valid'], tok, tok, h, wg, wu, wd)


def _combine_kernel(idx_ref, idx_next_ref, route_ref, xn_ref, gate_ref, y_hbm, o_ref, buf, sem):
    i = pl.program_id(0)
    cur = lax.rem(i, 2)

    @pl.when(i == 0)
    def _():
        _start_rows(idx_ref, 2 * TM, y_hbm, buf.at[0], sem.at[0])

    @pl.when(i + 1 < pl.num_programs(0))
    def _():
        _start_rows(idx_next_ref, 2 * TM, y_hbm, buf.at[1 - cur], sem.at[1 - cur])

    _wait_rows(idx_ref, 2 * TM, y_hbm, buf.at[cur], sem.at[cur])
    y = buf[cur, :TM, :] * route_ref[:, 2:3] + buf[cur, TM:, :] * route_ref[:, 3:4]
    o_ref[...] = xn_ref[...] + gate_ref[0] * y


def _combine(slots, route, xn, mod, y_sorted, tpb):
    t = xn.shape[0]
    n_tiles = t // TM
    return pl.pallas_call(
        _combine_kernel,
        grid=(n_tiles,),
        in_specs=[
            pl.BlockSpec((1, 1, 2 * TM), lambda i: (i, 0, 0), memory_space=pltpu.SMEM),
            pl.BlockSpec((1, 1, 2 * TM), lambda i: (jnp.minimum(i + 1, n_tiles - 1), 0, 0),
                         memory_space=pltpu.SMEM),
            pl.BlockSpec((TM, ROUTE_LANES), lambda i: (i, 0)),
            pl.BlockSpec((TM, D_MODEL), lambda i: (i, 0)),
            _mod_spec(5, tpb),
            pl.BlockSpec(memory_space=pl.ANY),
        ],
        out_specs=pl.BlockSpec((TM, D_MODEL), lambda i: (i, 0)),
        out_shape=jax.ShapeDtypeStruct((t, D_MODEL), F32),
        scratch_shapes=[pltpu.VMEM((2, 2 * TM, D_MODEL), F32), pltpu.SemaphoreType.DMA((2,))],
        compiler_params=_params("arbitrary"),
        name="combine",
    )(slots, slots, route, xn, mod, y_sorted)


def _dispatch(route):
    t = route.shape[0]
    flat_e = route[:, :2].astype(jnp.int32).reshape(-1)
    n_pairs = flat_e.shape[0]
    experts = jnp.arange(N_EXPERTS, dtype=jnp.int32)
    onehot = (flat_e[:, None] == experts[None, :]).astype(jnp.int32)
    csum = jnp.cumsum(onehot, axis=0)
    rank = jnp.sum(csum * onehot, axis=1) - 1
    counts = csum[-1]
    padded = (counts + MOE_TILE - 1) // MOE_TILE * MOE_TILE
    pend = jnp.cumsum(padded)
    slot = jnp.sum((pend - padded)[None, :] * onehot, axis=1) + rank
    n_tiles = -(-n_pairs // MOE_TILE) + N_EXPERTS
    n_used = (pend[-1] // MOE_TILE).astype(jnp.int32)
    tile_ids = jnp.arange(n_tiles, dtype=jnp.int32)
    used = tile_ids < n_used
    te = jnp.minimum(jnp.sum((pend[None, :] <= (tile_ids * MOE_TILE)[:, None]).astype(jnp.int32), axis=1),
                     N_EXPERTS - 1)
    prev = jnp.concatenate([jnp.full((1,), -1, jnp.int32), te[:-1]])
    first = jnp.logical_and(used, te != prev).astype(jnp.int32)
    par = (jnp.cumsum(first) - 1) % 2
    later = jnp.logical_and(experts[None, :] > experts[:, None], (counts > 0)[None, :])
    next_expert = jnp.min(jnp.where(later, experts[None, :], N_EXPERTS), axis=1)
    next_expert = jnp.where(next_expert == N_EXPERTS, -1, next_expert)
    tok = jnp.zeros((n_tiles * MOE_TILE,), jnp.int32).at[slot].set(jnp.arange(n_pairs, dtype=jnp.int32) // 2)
    slots = slot.reshape(t // TM, TM, 2).transpose(0, 2, 1).reshape(t // TM, 1, 2 * TM)
    seg_end = (pend - padded + counts)[te]
    valid = jnp.where(used, jnp.clip(seg_end - tile_ids * MOE_TILE, 0, MOE_TILE), 0)
    plan = dict(n_used=n_used.reshape(1), te=te.astype(jnp.int32), first=first,
                nxt=next_expert[te].astype(jnp.int32), par=par.astype(jnp.int32), valid=valid.astype(jnp.int32))
    return slots, tok.reshape(n_tiles, 1, MOE_TILE), plan


def _rope_tables(n_latent):
    rows = n_latent // GRID_W
    row = jnp.repeat(jnp.arange(rows), GRID_W).astype(F32)
    col = jnp.tile(jnp.arange(GRID_W), rows).astype(F32)
    n_freq = QK_ROPE // 4
    inv_freq = ROPE_BASE ** (-jnp.arange(n_freq, dtype=F32) / n_freq)
    ar, ac = row[:, None] * inv_freq[None, :], col[:, None] * inv_freq[None, :]
    zeros = jnp.zeros((n_latent, LANES - QK_ROPE), F32)
    cos = jnp.concatenate([jnp.cos(ar), jnp.cos(ar), jnp.cos(ac), jnp.cos(ac), zeros], axis=1)
    sin = jnp.concatenate([-jnp.sin(ar), jnp.sin(ar), -jnp.sin(ac), jnp.sin(ac), zeros], axis=1)
    ctx_cos = jnp.concatenate([jnp.ones((TM, QK_ROPE), F32), jnp.zeros((TM, LANES - QK_ROPE), F32)], axis=1)
    return (jnp.concatenate([ctx_cos, cos], axis=0),
            jnp.concatenate([jnp.zeros((TM, LANES), F32), sin], axis=0))


def _pad_cols(w, n):
    return jnp.concatenate([w, jnp.zeros(w.shape[:-1] + (n,), w.dtype)], axis=-1)


def _layer_params(l, w_in, q_a_norm, w_uq, kv_a_norm, w_ukv, q_norm, k_norm, conv_w, conv_b, lru_wa, lru_ba,
                  lru_wx, lru_bx, lru_lambda, out_norm, w_out):
    s0, s1, s2, s3 = Q_RANK, Q_RANK + KV_RANK, Q_RANK + KV_RANK + QK_ROPE, Q_RANK + KV_RANK + QK_ROPE + LRU_WIDTH
    w = w_in[l]
    w_in_p = jnp.concatenate([w[:, s2:s3], w[:, s3:], w[:, :s0], w[:, s0:s1],
                              _pad_cols(w[:, s1:s2], LANES - QK_ROPE)], axis=1).astype(BF16)
    wq = w_uq[l].reshape(Q_RANK, MLA_HEADS, QK_HEAD)
    wuq_p = _pad_cols(wq, QK_PAD - QK_HEAD).reshape(Q_RANK, MLA_HEADS * QK_PAD).astype(BF16)
    wkv = w_ukv[l].reshape(KV_RANK, MLA_HEADS, QK_NOPE + V_HEAD)
    wukv_p = jnp.concatenate([wkv[:, :, :QK_NOPE].reshape(KV_RANK, -1),
                              wkv[:, :, QK_NOPE:].reshape(KV_RANK, -1)], axis=1).astype(BF16)
    w_gate = [jnp.concatenate([lru_wa[l, d], lru_wx[l, d]], axis=-1).astype(BF16) for d in range(2)]
    return dict(
        w_in=w_in_p, wuq=wuq_p, wukv=wukv_p,
        qa=q_a_norm[l][None], kva=kv_a_norm[l][None],
        qn=_pad_cols(q_norm[l], QK_PAD - QK_HEAD)[None], kn=_pad_cols(k_norm[l], QK_PAD - QK_HEAD)[None],
        conv_w=conv_w[l], conv_b=conv_b[l][None], w_gate=w_gate,
        ba=[lru_ba[l, d][None] for d in range(2)], bx=[lru_bx[l, d][None] for d in range(2)],
        lam=[lru_lambda[l, d][None] for d in range(2)],
        out_norm=out_norm[l][None], w_out=w_out[l].astype(BF16),
    )


def kernel(x, c, ctx, c_ctx, ada_w, ada_b, w_in, q_a_norm, w_uq, kv_a_norm, w_ukv, q_norm, k_norm, conv_w, conv_b, lru_wa, lru_ba, lru_wx, lru_bx, lru_lambda, out_norm, w_out, router_w, router_b, moe_w_gate, moe_w_up, moe_w_down):
    batch, n, d = x.shape
    depth = ada_w.shape[0]
    assert d == D_MODEL and ctx.shape[1] == TM and n % ATTN_KV_CHUNK == 0 and n % GRID_W == 0
    assert 1 + batch <= SUBLANES
    tpb = (TM + n) // TM
    t = batch * tpb * TM

    cvec = jnp.concatenate([c_ctx[None], c, jnp.zeros((SUBLANES - 1 - batch, d), F32)], axis=0)
    mod_all = _adaln(cvec, ada_w, ada_b)
    cos_t, sin_t = _rope_tables(n)
    rw = _pad_cols(router_w, ROUTE_LANES - N_EXPERTS)
    rw_hi = rw.astype(BF16)
    rw_lo = (rw - rw_hi.astype(F32)).astype(BF16)
    rb = _pad_cols(router_b, ROUTE_LANES - N_EXPERTS)[None]

    xs = jnp.concatenate([ctx, x], axis=1).reshape(t, d)
    for l in range(depth):
        p = _layer_params(l, w_in, q_a_norm, w_uq, kv_a_norm, w_ukv, q_norm, k_norm, conv_w, conv_b, lru_wa,
                          lru_ba, lru_wx, lru_bx, lru_lambda, out_norm, w_out)
        mod = mod_all[l].reshape(SUBLANES * 6, 1, d)
        lru_in, mla_in = _inproj(xs, mod, p['w_in'], tpb)
        q, k, vt = _mla_prep(mla_in, p['qa'], p['kva'], p['qn'], p['kn'], cos_t, sin_t, p['wuq'], p['wukv'], tpb)
        attn = _attention(q, k, vt, batch, tpb)
        h_fwd = _lru(lru_in, p['conv_w'], p['conv_b'], p['w_gate'][0], p['ba'][0], p['bx'][0], p['lam'][0],
                     batch, tpb, False)
        rec = _lru(lru_in, p['conv_w'], p['conv_b'], p['w_gate'][1], p['ba'][1], p['bx'][1], p['lam'][1],
                   batch, tpb, True, h_fwd)
        xn, h2, route = _merge(attn, rec, xs, mod, p['out_norm'], p['w_out'], rw_hi, rw_lo, rb, tpb)
        slots, tok, plan = _dispatch(route)
        y_sorted = _experts(h2, tok, plan, moe_w_gate, moe_w_up, moe_w_down, l)
        xs = _combine(slots, route, xn, mod, y_sorted, tpb)
    return xs.reshape(batch, tpb * TM, d)[:, TM:]
```

```python
import functools

import jax
import jax.numpy as jnp
from jax import lax
from jax.experimental import pallas as pl
from jax.experimental.pallas import tpu as pltpu

F32 = jnp.float32
BF16 = jnp.bfloat16

D_MODEL = 2048
MLA_HEADS = 8
QK_NOPE = 128
QK_ROPE = 64
QK_HEAD = QK_NOPE + QK_ROPE
V_HEAD = 128
Q_RANK = 512
KV_RANK = 256
MLA_WIDTH = MLA_HEADS * V_HEAD
LRU_WIDTH = D_MODEL - MLA_WIDTH
LRU_HEADS = 8
LRU_HEAD_DIM = LRU_WIDTH // LRU_HEADS
CONV_W = 4
LRU_C = 8.0
N_EXPERTS = 32
N_GROUPS = 4
EXPERTS_PER_GROUP = N_EXPERTS // N_GROUPS
D_FF_EXPERT = 768
GRID_W = 64
ROPE_BASE = 10000.0
EPS = 1e-6
LOG2_E = 1.4426950408889634

LANES = 128
SUBLANES = 8
VMEM_LIMIT = 56 * 1024 * 1024

TM = 256
QK_PAD = 2 * LANES
MLA_COLS = Q_RANK + KV_RANK + LANES
ATTN_KV_CHUNK = 512
MOE_TILE = 256
MOE_FF_CHUNK = 256
ROUTE_LANES = LANES
DMA_UNROLL = 8
WEIGHT_DMA_PRIORITY = 1

_NT = (((1,), (1,)), ((), ()))


def _dot(a, b):
    return jnp.dot(a, b, preferred_element_type=F32)


def _rms(x, denom=None):
    d = x.shape[-1] if denom is None else denom
    ms = jnp.sum(x * x, axis=-1, keepdims=True) * (1.0 / d)
    return x * lax.rsqrt(ms + EPS)


def _params(*sem):
    return pltpu.CompilerParams(dimension_semantics=sem, vmem_limit_bytes=VMEM_LIMIT)


def _resident(shape):
    nd = len(shape)
    return pl.BlockSpec(shape, lambda *_: (0,) * nd, pipeline_mode=pl.Buffered(1))


def _adaln_kernel(c_ref, w_ref, b_ref, o_ref):
    s = jax.nn.silu(c_ref[...]).astype(BF16)
    o_ref[0] = _dot(s, w_ref[0].astype(BF16)) + b_ref[0]


def _adaln(cvec, ada_w, ada_b):
    depth, d, n6 = ada_w.shape
    rows = cvec.shape[0]
    tn = 1024
    return pl.pallas_call(
        _adaln_kernel,
        grid=(depth, n6 // tn),
        in_specs=[
            pl.BlockSpec((rows, d), lambda l, j: (0, 0)),
            pl.BlockSpec((1, d, tn), lambda l, j: (l, 0, j)),
            pl.BlockSpec((1, 1, tn), lambda l, j: (l, 0, j)),
        ],
        out_specs=pl.BlockSpec((1, rows, tn), lambda l, j: (l, 0, j)),
        out_shape=jax.ShapeDtypeStruct((depth, rows, n6), F32),
        compiler_params=_params("arbitrary", "arbitrary"),
        name="adaln",
    )(cvec, ada_w, ada_b.reshape(depth, 1, n6))


def _seg(i, tiles_per_batch):
    return jnp.where(i % tiles_per_batch == 0, 0, 1 + i // tiles_per_batch)


def _mod_spec(chunk, tiles_per_batch):
    return pl.BlockSpec((1, 1, D_MODEL), lambda i: (_seg(i, tiles_per_batch) * 6 + chunk, 0, 0))


def _swap_halves(x):
    lane = lax.broadcasted_iota(jnp.int32, x.shape, 1)
    up = pltpu.roll(x, LANES - 16, 1)
    down = pltpu.roll(x, 16, 1)
    return jnp.where((lane & 31) < 16, up, down)


def _inproj_kernel(x_ref, shift_ref, scale_ref, w_ref, qa_ref, kva_ref, qn_ref, kn_ref, cos_ref, sin_ref,
                   wuq_ref, wukv_ref, lru_ref, q_ref, k_ref, vt_ref):
    h = _rms(x_ref[...]) * (1.0 + scale_ref[0]) + shift_ref[0]
    hb = h.astype(BF16)
    p = _dot(hb, w_ref[:, 2 * LRU_WIDTH:])
    lru_ref[...] = _dot(hb, w_ref[:, :2 * LRU_WIDTH])

    cos = cos_ref[...]
    sin = sin_ref[...]
    scale = QK_HEAD ** -0.5 * LOG2_E

    def rope(x):
        return x * cos + _swap_halves(x) * sin

    cq = p[:, :Q_RANK]
    qf = _dot((_rms(cq) * qa_ref[...]).astype(BF16), wuq_ref[...])
    ckv = p[:, Q_RANK:Q_RANK + KV_RANK]
    kvf = _dot((_rms(ckv) * kva_ref[...]).astype(BF16), wukv_ref[...])
    kr = p[:, Q_RANK + KV_RANK:]
    k_rope = rope(_rms(kr, QK_ROPE) * kn_ref[:, QK_NOPE:]).astype(BF16)
    for h in range(MLA_HEADS):
        c0 = h * QK_PAD
        q_nope = _rms(qf[:, c0:c0 + QK_NOPE]) * qn_ref[:, :QK_NOPE]
        q_rope = rope(_rms(qf[:, c0 + QK_NOPE:c0 + QK_PAD], QK_ROPE) * qn_ref[:, QK_NOPE:])
        q_ref[h, :, :QK_NOPE] = (q_nope * scale).astype(BF16)
        q_ref[h, :, QK_NOPE:] = (q_rope * scale).astype(BF16)
        k_nope = _rms(kvf[:, h * QK_NOPE:(h + 1) * QK_NOPE]) * kn_ref[:, :QK_NOPE]
        k_ref[h, :, :QK_NOPE] = k_nope.astype(BF16)
        k_ref[h, :, QK_NOPE:] = k_rope
        v0 = MLA_HEADS * QK_NOPE + h * V_HEAD
        vt_ref[h] = kvf[:, v0:v0 + V_HEAD].T.astype(BF16)


def _inproj(x, mod, w_in_p, qa, kva, qn, kn, cos_t, sin_t, wuq_p, wukv_p, tpb):
    t = x.shape[0]
    row = lambda n: pl.BlockSpec((1, n), lambda i: (0, 0))
    return pl.pallas_call(
        _inproj_kernel,
        grid=(t // TM,),
        in_specs=[
            pl.BlockSpec((TM, D_MODEL), lambda i: (i, 0)),
            _mod_spec(0, tpb),
            _mod_spec(1, tpb),
            _resident(w_in_p.shape),
            row(Q_RANK), row(KV_RANK), row(QK_PAD), row(QK_PAD),
            pl.BlockSpec((TM, LANES), lambda i: (i % tpb, 0)),
            pl.BlockSpec((TM, LANES), lambda i: (i % tpb, 0)),
            _resident(wuq_p.shape),
            _resident(wukv_p.shape),
        ],
        out_specs=[
            pl.BlockSpec((TM, 2 * LRU_WIDTH), lambda i: (i, 0)),
            pl.BlockSpec((MLA_HEADS, TM, QK_PAD), lambda i: (0, i, 0)),
            pl.BlockSpec((MLA_HEADS, TM, QK_PAD), lambda i: (0, i, 0)),
            pl.BlockSpec((MLA_HEADS, V_HEAD, TM), lambda i: (0, 0, i)),
        ],
        out_shape=[
            jax.ShapeDtypeStruct((t, 2 * LRU_WIDTH), F32),
            jax.ShapeDtypeStruct((MLA_HEADS, t, QK_PAD), BF16),
            jax.ShapeDtypeStruct((MLA_HEADS, t, QK_PAD), BF16),
            jax.ShapeDtypeStruct((MLA_HEADS, V_HEAD, t), BF16),
        ],
        compiler_params=_params("arbitrary"),
        name="inproj",
    )(x, mod, mod, w_in_p, qa, kva, qn, kn, cos_t, sin_t, wuq_p, wukv_p)


def _column_reduce(x, op):
    return op(x.reshape(x.shape[0] // SUBLANES, SUBLANES, x.shape[1]), axis=0)


def _attn_kernel(q_ref, k_ref, vt_ref, o_ref, s0_ref, s1_ref, *, latent_chunks):
    is_ctx = pl.program_id(1) == 0
    ctx_only = ((0, TM),)
    every_key = ctx_only + latent_chunks

    def score_pass(j, s_ref, chunks):
        q = q_ref[j]
        m8 = None
        for start, size in chunks:
            s = lax.dot_general(k_ref[j, start:start + size, :], q, _NT, preferred_element_type=F32)
            s_ref[start:start + size, :] = s
            cm = _column_reduce(s, jnp.max)
            m8 = cm if m8 is None else jnp.maximum(m8, cm)
        return jnp.max(m8, axis=0, keepdims=True)

    def value_pass(j, m, s_ref, chunks):
        l8 = jnp.zeros((SUBLANES, TM), F32)
        acc = jnp.zeros((V_HEAD, TM), F32)
        for start, size in chunks:
            e = jnp.exp2(s_ref[start:start + size, :] - m)
            l8 = l8 + _column_reduce(e, jnp.sum)
            acc = acc + _dot(vt_ref[j, :, start:start + size], e.astype(BF16))
        l = jnp.sum(l8, axis=0, keepdims=True)
        o_ref[j] = (acc / l).T

    def all_heads(chunks):
        def pair(i, m_even):
            h = 2 * i
            m_odd = score_pass(h + 1, s1_ref, chunks)
            value_pass(h, m_even, s0_ref, chunks)
            m_even = score_pass(h + 2, s0_ref, chunks)
            value_pass(h + 1, m_odd, s1_ref, chunks)
            return m_even

        m_even = lax.fori_loop(0, MLA_HEADS // 2 - 1, pair, score_pass(0, s0_ref, chunks))
        m_odd = score_pass(MLA_HEADS - 1, s1_ref, chunks)
        value_pass(MLA_HEADS - 2, m_even, s0_ref, chunks)
        value_pass(MLA_HEADS - 1, m_odd, s1_ref, chunks)

    @pl.when(is_ctx)
    def _():
        all_heads(ctx_only)

    @pl.when(jnp.logical_not(is_ctx))
    def _():
        all_heads(every_key)


def _attention(q, k, vt, batch, tpb):
    t = q.shape[1]
    rows = tpb * TM
    n_chunks = (rows - TM) // ATTN_KV_CHUNK
    latent_chunks = tuple((TM + c * ATTN_KV_CHUNK, ATTN_KV_CHUNK) for c in range(n_chunks))
    once = pl.Buffered(1)
    return pl.pallas_call(
        functools.partial(_attn_kernel, latent_chunks=latent_chunks),
        grid=(batch, tpb),
        in_specs=[
            pl.BlockSpec((MLA_HEADS, TM, QK_PAD), lambda b, i: (0, b * tpb + i, 0)),
            pl.BlockSpec((MLA_HEADS, rows, QK_PAD), lambda b, i: (0, b, 0), pipeline_mode=once),
            pl.BlockSpec((MLA_HEADS, V_HEAD, rows), lambda b, i: (0, 0, b), pipeline_mode=once),
        ],
        out_specs=pl.BlockSpec((MLA_HEADS, TM, V_HEAD), lambda b, i: (0, b * tpb + i, 0)),
        out_shape=jax.ShapeDtypeStruct((MLA_HEADS, t, V_HEAD), F32),
        scratch_shapes=[pltpu.VMEM((rows, TM), F32), pltpu.VMEM((rows, TM), F32)],
        compiler_params=_params("arbitrary", "arbitrary"),
        name="attention",
    )(q, k, vt)


def _shift_rows(x, s, fill, reverse):
    rows = x.shape[0]
    pad = jnp.full((s, x.shape[1]), fill, x.dtype)
    if s % SUBLANES == 0:
        return jnp.concatenate([x[s:], pad], 0) if reverse else jnp.concatenate([pad, x[:rows - s]], 0)
    row = lax.broadcasted_iota(jnp.int32, x.shape, 0)
    if reverse:
        return jnp.where(row < rows - s, pltpu.roll(x, rows - s, 0), fill)
    return jnp.where(row >= s, pltpu.roll(x, s, 0), fill)


def _lru_kernel(*refs, reverse, tpb):
    if reverse:
        (x_ref, prev_ref, next_ref, cw_ref, cb_ref, w_ref, ba_ref, bx_ref, lam_ref, hf_ref, xg_ref,
         o_ref, carry_ref) = refs
    else:
        (x_ref, prev_ref, next_ref, cw_ref, cb_ref, w_ref, ba_ref, bx_ref, lam_ref,
         o_ref, carry_ref) = refs
    step = pl.program_id(1)
    tile = _lru_tile(step, tpb, reverse)

    @pl.when(step == 0)
    def _():
        carry_ref[...] = jnp.zeros_like(carry_ref)

    x = x_ref[...]
    row = lax.broadcasted_iota(jnp.int32, x.shape, 0)
    at_start = tile <= 1
    at_end = jnp.logical_or(tile == 0, tile == tpb - 1)
    before1 = jnp.where(at_start, 0.0, prev_ref[SUBLANES - 1:SUBLANES, :])
    before2 = jnp.where(at_start, 0.0, prev_ref[SUBLANES - 2:SUBLANES - 1, :])
    after1 = jnp.where(at_end, 0.0, next_ref[0:1, :])
    x_m1 = jnp.where(row == 0, before1, pltpu.roll(x, 1, 0))
    x_m2 = jnp.where(row == 0, before2, jnp.where(row == 1, before1, pltpu.roll(x, 2, 0)))
    x_p1 = jnp.where(row == TM - 1, after1, pltpu.roll(x, TM - 1, 0))
    u = cb_ref[...] + (((x_m2 * cw_ref[0:1, :] + x_m1 * cw_ref[1:2, :]) + x * cw_ref[2:3, :])
                       + x_p1 * cw_ref[3:4, :])

    decay = -LRU_C * jax.nn.softplus(-lam_ref[...])
    for h in range(LRU_HEADS):
        c0 = h * LRU_HEAD_DIM
        sl = slice(c0, c0 + LRU_HEAD_DIM)
        uh = u[:, sl]
        z = _dot(uh.astype(BF16), w_ref[h])
        r = jax.nn.sigmoid(z[:, :LRU_HEAD_DIM] + ba_ref[:, sl])
        i = jax.nn.sigmoid(z[:, LRU_HEAD_DIM:] + bx_ref[:, sl])
        log_a = r * decay[:, sl]
        a = jnp.exp(log_a)
        b = jnp.sqrt(-jnp.tanh(log_a) * (1.0 + a * a)) * (i * uh)
        s = 1
        while s < TM:
            b = a * _shift_rows(b, s, 0.0, reverse) + b
            a = a * _shift_rows(a, s, 1.0, reverse)
            s *= 2
        hcur = b + a * carry_ref[0:1, sl]
        edge = 0 if reverse else TM - 1
        carry_ref[0:1, sl] = hcur[edge:edge + 1, :]
        if reverse:
            o_ref[:, sl] = jax.nn.gelu(xg_ref[:, sl]) * (hf_ref[:, sl] + hcur)
        else:
            o_ref[:, sl] = hcur


def _lru_tile(step, tpb, reverse):
    if not reverse:
        return step
    return jnp.where(step == 0, 0, tpb - step)


def _lru(lru_in, conv_w, conv_b, w_gate, ba, bx, lam, batch, tpb, reverse, h_fwd=None):
    t = lru_in.shape[0]
    halo_blocks = t // SUBLANES
    per_tile = TM // SUBLANES

    def tile_idx(b, s):
        return b * tpb + _lru_tile(s, tpb, reverse)

    row = lambda n: pl.BlockSpec((1, n), lambda b, s: (0, 0))
    in_specs = [
        pl.BlockSpec((TM, LRU_WIDTH), lambda b, s: (tile_idx(b, s), 0)),
        pl.BlockSpec((SUBLANES, LRU_WIDTH), lambda b, s: (jnp.maximum(tile_idx(b, s) * per_tile - 1, 0), 0)),
        pl.BlockSpec((SUBLANES, LRU_WIDTH),
                     lambda b, s: (jnp.minimum((tile_idx(b, s) + 1) * per_tile, halo_blocks - 1), 0)),
        pl.BlockSpec((CONV_W, LRU_WIDTH), lambda b, s: (0, 0)),
        row(LRU_WIDTH),
        pl.BlockSpec((LRU_HEADS, LRU_HEAD_DIM, 2 * LRU_HEAD_DIM), lambda b, s: (0, 0, 0)),
        row(LRU_WIDTH), row(LRU_WIDTH), row(LRU_WIDTH),
    ]
    args = [lru_in, lru_in, lru_in, conv_w, conv_b, w_gate, ba, bx, lam]
    if reverse:
        in_specs += [
            pl.BlockSpec((TM, LRU_WIDTH), lambda b, s: (tile_idx(b, s), 0)),
            pl.BlockSpec((TM, LRU_WIDTH), lambda b, s: (tile_idx(b, s), 1)),
        ]
        args += [h_fwd, lru_in]
    return pl.pallas_call(
        functools.partial(_lru_kernel, reverse=reverse, tpb=tpb),
        grid=(batch, tpb),
        in_specs=in_specs,
        out_specs=pl.BlockSpec((TM, LRU_WIDTH), lambda b, s: (tile_idx(b, s), 0)),
        out_shape=jax.ShapeDtypeStruct((t, LRU_WIDTH), F32),
        scratch_shapes=[pltpu.VMEM((SUBLANES, LRU_WIDTH), F32)],
        compiler_params=_params("arbitrary", "arbitrary"),
        name="lru_rev" if reverse else "lru_fwd",
    )(*args)


def _first_index(mask, lane_f):
    return jnp.min(jnp.where(mask, lane_f, float(ROUTE_LANES)), axis=-1, keepdims=True)


def _route(logits, rb):
    lane = lax.broadcasted_iota(jnp.int32, logits.shape, 1)
    lane_f = lane.astype(F32)
    group = lane >> 3
    scores = jax.nn.sigmoid(logits)
    biased = scores + rb
    neg = -jnp.inf
    best = e1 = e2 = None
    for g in range(N_GROUPS):
        vg = jnp.where(group == g, biased, neg)
        m1 = jnp.max(vg, axis=-1, keepdims=True)
        i1 = _first_index(vg == m1, lane_f)
        vg2 = jnp.where(lane_f == i1, neg, vg)
        m2 = jnp.max(vg2, axis=-1, keepdims=True)
        i2 = _first_index(vg2 == m2, lane_f)
        gs = m1 + m2
        if g == 0:
            best, e1, e2 = gs, i1, i2
        else:
            better = gs > best
            best = jnp.where(better, gs, best)
            e1 = jnp.where(better, i1, e1)
            e2 = jnp.where(better, i2, e2)
    s1 = jnp.sum(jnp.where(lane_f == e1, scores, 0.0), axis=-1, keepdims=True)
    s2 = jnp.sum(jnp.where(lane_f == e2, scores, 0.0), axis=-1, keepdims=True)
    tot = s1 + s2
    out = jnp.where(lane == 0, e1, jnp.where(lane == 1, e2, jnp.where(lane == 2, s1 / tot, s2 / tot)))
    return jnp.where(lane < 4, out, 0.0)


def _merge_kernel(attn_ref, rec_ref, x_ref, on_ref, gate_ref, shift_ref, scale_ref, wout_ref,
                  rwh_ref, rwl_ref, rb_ref, xn_ref, h_ref, route_ref):
    attn = jnp.concatenate([attn_ref[h] for h in range(MLA_HEADS)], axis=1)
    ya = (_rms(attn) * on_ref[:, :MLA_WIDTH]).astype(BF16)
    yr = (_rms(rec_ref[...]) * on_ref[:, MLA_WIDTH:]).astype(BF16)
    y = _dot(ya, wout_ref[:MLA_WIDTH, :]) + _dot(yr, wout_ref[MLA_WIDTH:, :])
    xn = x_ref[...] + gate_ref[0] * y
    xn_ref[...] = xn
    h = _rms(xn) * (1.0 + scale_ref[0]) + shift_ref[0]
    h_ref[...] = h
    h_hi = h.astype(BF16)
    h_lo = (h - h_hi.astype(F32)).astype(BF16)
    logits = (_dot(h_hi, rwh_ref[...]) + _dot(h_lo, rwh_ref[...])) + _dot(h_hi, rwl_ref[...])
    route_ref[...] = _route(logits, rb_ref[...])


def _merge(attn, rec, x, mod, out_norm, w_out, rw_hi, rw_lo, rb, tpb):
    t = x.shape[0]
    tile = lambda n: pl.BlockSpec((TM, n), lambda i: (i, 0))
    return pl.pallas_call(
        _merge_kernel,
        grid=(t // TM,),
        in_specs=[
            pl.BlockSpec((MLA_HEADS, TM, V_HEAD), lambda i: (0, i, 0)), tile(LRU_WIDTH), tile(D_MODEL),
            pl.BlockSpec((1, D_MODEL), lambda i: (0, 0)),
            _mod_spec(2, tpb), _mod_spec(3, tpb), _mod_spec(4, tpb),
            _resident(w_out.shape), _resident(rw_hi.shape), _resident(rw_lo.shape),
            pl.BlockSpec((1, ROUTE_LANES), lambda i: (0, 0)),
        ],
        out_specs=[tile(D_MODEL), tile(D_MODEL), tile(ROUTE_LANES)],
        out_shape=[
            jax.ShapeDtypeStruct((t, D_MODEL), F32),
            jax.ShapeDtypeStruct((t, D_MODEL), F32),
            jax.ShapeDtypeStruct((t, ROUTE_LANES), F32),
        ],
        compiler_params=_params("arbitrary"),
        name="merge",
    )(attn, rec, x, out_norm, mod, mod, mod, w_out, rw_hi, rw_lo, rb)


def _row_copy(idx_ref, src_ref, dst_ref, sem, r):
    return pltpu.make_async_copy(src_ref.at[pl.ds(idx_ref[0, 0, r], 1)], dst_ref.at[pl.ds(r, 1)], sem)


def _start_rows(idx_ref, n, src_ref, dst_ref, sem):
    def body(g, c):
        for u in range(DMA_UNROLL):
            _row_copy(idx_ref, src_ref, dst_ref, sem, g * DMA_UNROLL + u).start()
        return c

    lax.fori_loop(0, (n + DMA_UNROLL - 1) // DMA_UNROLL, body, 0)


def _wait_rows(idx_ref, n, src_ref, dst_ref, sem):
    def body(g, c):
        for u in range(DMA_UNROLL):
            _row_copy(idx_ref, src_ref, dst_ref, sem, g * DMA_UNROLL + u).wait()
        return c

    lax.fori_loop(0, (n + DMA_UNROLL - 1) // DMA_UNROLL, body, 0)


def _expert_kernel(used_ref, te_ref, first_ref, nxt_ref, par_ref, valid_ref, tok_ref, tok_next_ref, h_hbm,
                   wg_hbm, wu_hbm, wd_hbm, o_ref, x_buf, wg_buf, wu_buf, wd_buf, xsem, wsem, *, layer):
    j = pl.program_id(0)
    used = used_ref[0]

    def fetch(expert, slot):
        return (pltpu.make_async_copy(wg_hbm.at[layer, expert], wg_buf.at[slot], wsem.at[slot, 0]),
                pltpu.make_async_copy(wu_hbm.at[layer, expert], wu_buf.at[slot], wsem.at[slot, 1]),
                pltpu.make_async_copy(wd_hbm.at[layer, expert], wd_buf.at[slot], wsem.at[slot, 2]))

    @pl.when(j < used)
    def _():
        slot = par_ref[j]
        xslot = lax.rem(j, 2)

        @pl.when(j == 0)
        def _():
            x_buf[...] = jnp.zeros_like(x_buf)
            _start_rows(tok_ref, valid_ref[0], h_hbm, x_buf.at[0], xsem.at[0])
            for c in fetch(te_ref[0], 0):
                c.start(priority=WEIGHT_DMA_PRIORITY)

        @pl.when(j + 1 < used)
        def _():
            _start_rows(tok_next_ref, valid_ref[j + 1], h_hbm, x_buf.at[1 - xslot], xsem.at[1 - xslot])

        @pl.when(first_ref[j] == 1)
        def _():
            for c in fetch(te_ref[j], slot):
                c.wait()

            @pl.when(nxt_ref[j] >= 0)
            def _():
                for c in fetch(nxt_ref[j], 1 - slot):
                    c.start(priority=WEIGHT_DMA_PRIORITY)

        _wait_rows(tok_ref, valid_ref[j], h_hbm, x_buf.at[xslot], xsem.at[xslot])
        x = x_buf[xslot].astype(BF16)
        acc = jnp.zeros((MOE_TILE, D_MODEL), F32)
        for c in range(D_FF_EXPERT // MOE_FF_CHUNK):
            cs = slice(c * MOE_FF_CHUNK, (c + 1) * MOE_FF_CHUNK)
            g = _dot(x, wg_buf[slot, :, cs].astype(BF16))
            u = _dot(x, wu_buf[slot, :, cs].astype(BF16))
            act = (jax.nn.silu(g) * u).astype(BF16)
            acc = acc + _dot(act, wd_buf[slot, cs, :].astype(BF16))
        o_ref[...] = acc

    @pl.when(j >= used)
    def _():
        o_ref[...] = jnp.zeros_like(o_ref)


def _experts(h, tok, plan, wg, wu, wd, layer):
    n_tiles = tok.shape[0]
    grid_spec = pltpu.PrefetchScalarGridSpec(
        num_scalar_prefetch=6,
        grid=(n_tiles,),
        in_specs=[
            pl.BlockSpec((1, 1, MOE_TILE), lambda j, *_: (j, 0, 0), memory_space=pltpu.SMEM),
            pl.BlockSpec((1, 1, MOE_TILE), lambda j, *_: (jnp.minimum(j + 1, n_tiles - 1), 0, 0),
                         memory_space=pltpu.SMEM),
            pl.BlockSpec(memory_space=pl.ANY),
            pl.BlockSpec(memory_space=pl.ANY),
            pl.BlockSpec(memory_space=pl.ANY),
            pl.BlockSpec(memory_space=pl.ANY),
        ],
        out_specs=pl.BlockSpec((MOE_TILE, D_MODEL), lambda j, *_: (j, 0)),
        scratch_shapes=[
            pltpu.VMEM((2, MOE_TILE, D_MODEL), F32),
            pltpu.VMEM((2, D_MODEL, D_FF_EXPERT), F32),
            pltpu.VMEM((2, D_MODEL, D_FF_EXPERT), F32),
            pltpu.VMEM((2, D_FF_EXPERT, D_MODEL), F32),
            pltpu.SemaphoreType.DMA((2,)),
            pltpu.SemaphoreType.DMA((2, 3)),
        ],
    )
    return pl.pallas_call(
        functools.partial(_expert_kernel, layer=layer),
        grid_spec=grid_spec,
        out_shape=jax.ShapeDtypeStruct((n_tiles * MOE_TILE, D_MODEL), F32),
        compiler_params=_params("arbitrary"),
        name="experts",
    )(plan['n_used'], plan['te'], plan['first'], plan['nxt'], plan['par'], plan['valid'], tok, tok, h, wg, wu, wd)


def _combine_kernel(idx_ref, idx_next_ref, route_ref, xn_ref, gate_ref, y_hbm, o_ref, buf, sem):
    i = pl.program_id(0)
    cur = lax.rem(i, 2)

    @pl.when(i == 0)
    def _():
        _start_rows(idx_ref, 2 * TM, y_hbm, buf.at[0], sem.at[0])

    @pl.when(i + 1 < pl.num_programs(0))
    def _():
        _start_rows(idx_next_ref, 2 * TM, y_hbm, buf.at[1 - cur], sem.at[1 - cur])

    _wait_rows(idx_ref, 2 * TM, y_hbm, buf.at[cur], sem.at[cur])
    y = buf[cur, :TM, :] * route_ref[:, 2:3] + buf[cur, TM:, :] * route_ref[:, 3:4]
    o_ref[...] = xn_ref[...] + gate_ref[0] * y


def _combine(slots, route, xn, mod, y_sorted, tpb):
    t = xn.shape[0]
    n_tiles = t // TM
    return pl.pallas_call(
        _combine_kernel,
        grid=(n_tiles,),
        in_specs=[
            pl.BlockSpec((1, 1, 2 * TM), lambda i: (i, 0, 0), memory_space=pltpu.SMEM),
            pl.BlockSpec((1, 1, 2 * TM), lambda i: (jnp.minimum(i + 1, n_tiles - 1), 0, 0),
                         memory_space=pltpu.SMEM),
            pl.BlockSpec((TM, ROUTE_LANES), lambda i: (i, 0)),
            pl.BlockSpec((TM, D_MODEL), lambda i: (i, 0)),
            _mod_spec(5, tpb),
            pl.BlockSpec(memory_space=pl.ANY),
        ],
        out_specs=pl.BlockSpec((TM, D_MODEL), lambda i: (i, 0)),
        out_shape=jax.ShapeDtypeStruct((t, D_MODEL), F32),
        scratch_shapes=[pltpu.VMEM((2, 2 * TM, D_MODEL), F32), pltpu.SemaphoreType.DMA((2,))],
        compiler_params=_params("arbitrary"),
        name="combine",
    )(slots, slots, route, xn, mod, y_sorted)


def _dispatch(route):
    t = route.shape[0]
    flat_e = route[:, :2].astype(jnp.int32).reshape(-1)
    n_pairs = flat_e.shape[0]
    experts = jnp.arange(N_EXPERTS, dtype=jnp.int32)
    onehot = (flat_e[:, None] == experts[None, :]).astype(jnp.int32)
    csum = jnp.cumsum(onehot, axis=0)
    rank = jnp.sum(csum * onehot, axis=1) - 1
    counts = csum[-1]
    padded = (counts + MOE_TILE - 1) // MOE_TILE * MOE_TILE
    pend = jnp.cumsum(padded)
    slot = jnp.sum((pend - padded)[None, :] * onehot, axis=1) + rank
    n_tiles = -(-n_pairs // MOE_TILE) + N_EXPERTS
    n_used = (pend[-1] // MOE_TILE).astype(jnp.int32)
    tile_ids = jnp.arange(n_tiles, dtype=jnp.int32)
    used = tile_ids < n_used
    te = jnp.minimum(jnp.sum((pend[None, :] <= (tile_ids * MOE_TILE)[:, None]).astype(jnp.int32), axis=1),
                     N_EXPERTS - 1)
    prev = jnp.concatenate([jnp.full((1,), -1, jnp.int32), te[:-1]])
    first = jnp.logical_and(used, te != prev).astype(jnp.int32)
    par = (jnp.cumsum(first) - 1) % 2
    later = jnp.logical_and(experts[None, :] > experts[:, None], (counts > 0)[None, :])
    next_expert = jnp.min(jnp.where(later, experts[None, :], N_EXPERTS), axis=1)
    next_expert = jnp.where(next_expert == N_EXPERTS, -1, next_expert)
    tok = jnp.zeros((n_tiles * MOE_TILE,), jnp.int32).at[slot].set(
        jnp.arange(n_pairs, dtype=jnp.int32) // 2, unique_indices=True)
    slots = slot.reshape(t // TM, TM, 2).transpose(0, 2, 1).reshape(t // TM, 1, 2 * TM)
    seg_end = (pend - padded + counts)[te]
    valid = jnp.where(used, jnp.clip(seg_end - tile_ids * MOE_TILE, 0, MOE_TILE), 0)
    plan = dict(n_used=n_used.reshape(1), te=te.astype(jnp.int32), first=first,
                nxt=next_expert[te].astype(jnp.int32), par=par.astype(jnp.int32), valid=valid.astype(jnp.int32))
    return slots, tok.reshape(n_tiles, 1, MOE_TILE), plan


def _rope_tables(n_latent):
    rows = n_latent // GRID_W
    row = jnp.repeat(jnp.arange(rows), GRID_W).astype(F32)
    col = jnp.tile(jnp.arange(GRID_W), rows).astype(F32)
    n_freq = QK_ROPE // 4
    inv_freq = ROPE_BASE ** (-jnp.arange(n_freq, dtype=F32) / n_freq)
    ar, ac = row[:, None] * inv_freq[None, :], col[:, None] * inv_freq[None, :]
    zeros = jnp.zeros((n_latent, LANES - QK_ROPE), F32)
    cos = jnp.concatenate([jnp.cos(ar), jnp.cos(ar), jnp.cos(ac), jnp.cos(ac), zeros], axis=1)
    sin = jnp.concatenate([-jnp.sin(ar), jnp.sin(ar), -jnp.sin(ac), jnp.sin(ac), zeros], axis=1)
    ctx_cos = jnp.concatenate([jnp.ones((TM, QK_ROPE), F32), jnp.zeros((TM, LANES - QK_ROPE), F32)], axis=1)
    return (jnp.concatenate([ctx_cos, cos], axis=0),
            jnp.concatenate([jnp.zeros((TM, LANES), F32), sin], axis=0))


def _pad_cols(w, n):
    return jnp.concatenate([w, jnp.zeros(w.shape[:-1] + (n,), w.dtype)], axis=-1)


def _layer_params(l, w_in, q_a_norm, w_uq, kv_a_norm, w_ukv, q_norm, k_norm, conv_w, conv_b, lru_wa, lru_ba,
                  lru_wx, lru_bx, lru_lambda, out_norm, w_out):
    s0, s1, s2, s3 = Q_RANK, Q_RANK + KV_RANK, Q_RANK + KV_RANK + QK_ROPE, Q_RANK + KV_RANK + QK_ROPE + LRU_WIDTH
    w = w_in[l]
    w_in_p = jnp.concatenate([w[:, s2:s3], w[:, s3:], w[:, :s0], w[:, s0:s1],
                              _pad_cols(w[:, s1:s2], LANES - QK_ROPE)], axis=1).astype(BF16)
    wq = w_uq[l].reshape(Q_RANK, MLA_HEADS, QK_HEAD)
    wuq_p = _pad_cols(wq, QK_PAD - QK_HEAD).reshape(Q_RANK, MLA_HEADS * QK_PAD).astype(BF16)
    wkv = w_ukv[l].reshape(KV_RANK, MLA_HEADS, QK_NOPE + V_HEAD)
    wukv_p = jnp.concatenate([wkv[:, :, :QK_NOPE].reshape(KV_RANK, -1),
                              wkv[:, :, QK_NOPE:].reshape(KV_RANK, -1)], axis=1).astype(BF16)
    w_gate = [jnp.concatenate([lru_wa[l, d], lru_wx[l, d]], axis=-1).astype(BF16) for d in range(2)]
    return dict(
        w_in=w_in_p, wuq=wuq_p, wukv=wukv_p,
        qa=q_a_norm[l][None], kva=kv_a_norm[l][None],
        qn=_pad_cols(q_norm[l], QK_PAD - QK_HEAD)[None], kn=_pad_cols(k_norm[l], QK_PAD - QK_HEAD)[None],
        conv_w=conv_w[l], conv_b=conv_b[l][None], w_gate=w_gate,
        ba=[lru_ba[l, d][None] for d in range(2)], bx=[lru_bx[l, d][None] for d in range(2)],
        lam=[lru_lambda[l, d][None] for d in range(2)],
        out_norm=out_norm[l][None], w_out=w_out[l].astype(BF16),
    )


def kernel(x, c, ctx, c_ctx, ada_w, ada_b, w_in, q_a_norm, w_uq, kv_a_norm, w_ukv, q_norm, k_norm, conv_w, conv_b, lru_wa, lru_ba, lru_wx, lru_bx, lru_lambda, out_norm, w_out, router_w, router_b, moe_w_gate, moe_w_up, moe_w_down):
    batch, n, d = x.shape
    depth = ada_w.shape[0]
    assert d == D_MODEL and ctx.shape[1] == TM and n % ATTN_KV_CHUNK == 0 and n % GRID_W == 0
    assert 1 + batch <= SUBLANES
    tpb = (TM + n) // TM
    t = batch * tpb * TM

    cvec = jnp.concatenate([c_ctx[None], c, jnp.zeros((SUBLANES - 1 - batch, d), F32)], axis=0)
    mod_all = _adaln(cvec, ada_w, ada_b)
    cos_t, sin_t = _rope_tables(n)
    rw = _pad_cols(router_w, ROUTE_LANES - N_EXPERTS)
    rw_hi = rw.astype(BF16)
    rw_lo = (rw - rw_hi.astype(F32)).astype(BF16)
    rb = _pad_cols(router_b, ROUTE_LANES - N_EXPERTS)[None]

    xs = jnp.concatenate([ctx, x], axis=1).reshape(t, d)
    for l in range(depth):
        p = _layer_params(l, w_in, q_a_norm, w_uq, kv_a_norm, w_ukv, q_norm, k_norm, conv_w, conv_b, lru_wa,
                          lru_ba, lru_wx, lru_bx, lru_lambda, out_norm, w_out)
        mod = mod_all[l].reshape(SUBLANES * 6, 1, d)
        lru_in, q, k, vt = _inproj(xs, mod, p['w_in'], p['qa'], p['kva'], p['qn'], p['kn'], cos_t, sin_t,
                                   p['wuq'], p['wukv'], tpb)
        attn = _attention(q, k, vt, batch, tpb)
        h_fwd = _lru(lru_in, p['conv_w'], p['conv_b'], p['w_gate'][0], p['ba'][0], p['bx'][0], p['lam'][0],
                     batch, tpb, False)
        rec = _lru(lru_in, p['conv_w'], p['conv_b'], p['w_gate'][1], p['ba'][1], p['bx'][1], p['lam'][1],
                   batch, tpb, True, h_fwd)
        xn, h2, route = _merge(attn, rec, xs, mod, p['out_norm'], p['w_out'], rw_hi, rw_lo, rb, tpb)
        slots, tok, plan = _dispatch(route)
        y_sorted = _experts(h2, tok, plan, moe_w_gate, moe_w_up, moe_w_down, l)
        xs = _combine(slots, route, xn, mod, y_sorted, tpb)
    return xs.reshape(batch, tpb * TM, d)[:, TM:]
```
